```python
import math
import jax, jax.numpy as jnp
from jax import lax
import numpy as np

D_MODEL = 2048
BATCH = 8
SEQ = 4096
DEPTH = 4

MIX_WIDTH = D_MODEL
GROUP_WIDTH = MIX_WIDTH // 4

RET_HEADS = 4
RET_DK = GROUP_WIDTH // (2 * RET_HEADS)
RET_DV = 2 * RET_DK
RET_CHUNK = 128

SSD_INNER = GROUP_WIDTH
SSD_HEADDIM = 64
SSD_HEADS = SSD_INNER // SSD_HEADDIM
SSD_GROUPS = 2
SSD_STATE = 128
SSD_CONV = 4
SSD_CHUNK = 128
SSD_CONV_DIM = SSD_INNER + 2 * SSD_GROUPS * SSD_STATE
DT_MIN = 1e-3
DT_MAX = 1e-1

DIFF_HEADS = 4
DIFF_HD = GROUP_WIDTH // (2 * DIFF_HEADS)
DIFF_QBLOCK = 128

DIL_HEADS = 8
DIL_HD = GROUP_WIDTH // DIL_HEADS
DIL_PATTERNS = ((128, 1), (512, 4), (2048, 16))

FFN_HIDDEN = -(-8 * D_MODEL // (3 * 256)) * 256

IN_SPLITS = (
    RET_HEADS * RET_DK,
    RET_HEADS * RET_DK,
    RET_HEADS * RET_DV,
    GROUP_WIDTH,
    SSD_INNER,
    SSD_CONV_DIM,
    SSD_HEADS,
    2 * DIFF_HEADS * DIFF_HD,
    2 * DIFF_HEADS * DIFF_HD,
    2 * DIFF_HEADS * DIFF_HD,
    DIL_HEADS * DIL_HD,
    DIL_HEADS * DIL_HD,
    DIL_HEADS * DIL_HD,
)
IN_COLS = sum(IN_SPLITS)
NORM_EPS = 1e-6

kernel_name = 'hybrid_parallel_head_group_block'


def rms_norm(x, g, eps=NORM_EPS):
    xf = x.astype(jnp.float32)
    y = xf * lax.rsqrt(jnp.mean(xf * xf, axis=-1, keepdims=True) + eps)
    return (y * g.astype(jnp.float32)).astype(x.dtype)


def alibi_slopes(n):
    return jnp.exp2(-8.0 * (jnp.arange(n, dtype=jnp.float32) + 1.0) / n)


def causal_depthwise_conv(x, w, b):
    K = w.shape[1]
    y = lax.conv_general_dilated(
        x, jnp.transpose(w)[:, None, :].astype(x.dtype), window_strides=(1,),
        padding=((K - 1, 0),), dimension_numbers=('NWC', 'WIO', 'NWC'),
        feature_group_count=x.shape[-1])
    return y + b.astype(x.dtype)


def retention(q, k, v, gain):
    Bsz, S, H, dk = q.shape
    dv = v.shape[-1]
    C = RET_CHUNK
    n = S // C
    dt = q.dtype
    log_gamma = jnp.log1p(-jnp.exp2(-5.0 - jnp.arange(H, dtype=jnp.float32)))
    pos = jnp.arange(C, dtype=jnp.float32)
    rel = pos[:, None] - pos[None, :]
    intra = jnp.where(rel >= 0, jnp.exp(log_gamma[:, None, None] * jnp.maximum(rel, 0.0)), 0.0).astype(dt)
    zeta = jnp.exp(log_gamma[None, :] * (C - 1.0 - pos)[:, None]).astype(dt)
    xi = jnp.exp(log_gamma[None, :] * (pos + 1.0)[:, None]).astype(dt)
    chunk_decay = jnp.exp(log_gamma * C).astype(dt)
    q = q.reshape(Bsz, n, C, H, dk)
    k = (k * (dk ** -0.5)).reshape(Bsz, n, C, H, dk)
    v = v.reshape(Bsz, n, C, H, dv)
    scores = jnp.einsum('bnihd,bnjhd->bnhij', q, k) * intra
    y_intra = jnp.einsum('bnhij,bnjhe->bnihe', scores, v)
    kv = jnp.einsum('bnjhd,jh,bnjhe->bnhde', k, zeta, v)

    def step(state, kv_c):
        return state * chunk_decay[None, :, None, None] + kv_c, state

    state0 = jnp.zeros((Bsz, H, dk, dv), dt)
    _, prev = lax.scan(step, state0, jnp.moveaxis(kv, 1, 0))
    prev = jnp.moveaxis(prev, 0, 1)
    y_cross = jnp.einsum('bnihd,bnhde->bnihe', q, prev) * xi[None, None, :, :, None]
    y = (y_intra + y_cross).reshape(Bsz, S, H, dv)
    return rms_norm(y, gain.reshape(H, dv))


def ssd_chunked_scan(xdt, a, bm, cm):
    Bsz, S, H, P = xdt.shape
    G, N = bm.shape[2], bm.shape[3]
    R = H // G
    L = SSD_CHUNK
    n = S // L
    dt = xdt.dtype
    x = xdt.reshape(Bsz, n, L, G, R, P)
    bm = bm.reshape(Bsz, n, L, G, N)
    cm = cm.reshape(Bsz, n, L, G, N)
    a = a.astype(jnp.float32).reshape(Bsz, n, L, G, R).transpose(0, 3, 4, 1, 2)
    a_cum = jnp.cumsum(a, axis=-1)
    causal = jnp.tril(jnp.ones((L, L), dtype=bool))
    seg = a_cum[..., :, None] - a_cum[..., None, :]
    decay_in = jnp.where(causal, jnp.exp(jnp.where(causal, seg, 0.0)), 0.0).astype(dt)
    cb = jnp.einsum('bclgn,bcsgn->bgcls', cm, bm)
    y_diag = jnp.einsum('bgcls,bgrcls,bcsgrp->bclgrp', cb, decay_in, x)
    decay_to_end = jnp.exp(a_cum[..., -1:] - a_cum).astype(dt)
    states = jnp.einsum('bcsgn,bgrcs,bcsgrp->bcgrpn', bm, decay_to_end, x)
    chunk_decay = jnp.exp(a_cum[..., -1]).astype(dt)

    def step(h, inp):
        st, dec = inp
        return h * dec[..., None, None] + st, h

    h0 = jnp.zeros((Bsz, G, R, P, N), dt)
    _, h_prev = lax.scan(step, h0, (jnp.moveaxis(states, 1, 0), jnp.moveaxis(chunk_decay, -1, 0)))
    h_prev = jnp.moveaxis(h_prev, 0, 1)
    decay_from_start = jnp.exp(a_cum).astype(dt)
    y_off = jnp.einsum('bclgn,bcgrpn,bgrcl->bclgrp', cm, h_prev, decay_from_start)
    return (y_diag + y_off).reshape(Bsz, S, H, P)


def diff_attention(q, k, v, lam, sub_gain, lam_init, slopes):
    Bsz, S, H, _, d = q.shape
    QB = DIFF_QBLOCK
    nb = S // QB
    scale = d ** -0.5
    q_blocks = q.reshape(Bsz, nb, QB, H, 2, d).transpose(1, 0, 2, 3, 4, 5)
    kpos = jnp.arange(S)

    def one_block(args):
        qb, bi = args
        qpos = bi * QB + jnp.arange(QB)
        dist = qpos[:, None] - kpos[None, :]
        bias = -slopes[:, None, None] * dist.astype(jnp.float32)
        s = jnp.einsum('bqhcd,bkhcd->bhcqk', qb, k).astype(jnp.float32) * scale + bias[None, :, None]
        s = jnp.where(dist >= 0, s, -jnp.inf)
        p = jax.nn.softmax(s, axis=-1)
        att = (p[:, :, 0] - lam * p[:, :, 1]).astype(v.dtype)
        return jnp.einsum('bhqk,bkhe->bqhe', att, v)

    out = lax.map(one_block, (q_blocks, jnp.arange(nb)))
    out = out.transpose(1, 0, 2, 3, 4).reshape(Bsz, S, H, 2 * d)
    return rms_norm(out, sub_gain) * (1.0 - lam_init)


def dilated_branch(q, k, v, slopes, window, dilation):
    Bsz, S, H, d = q.shape
    W = window // dilation
    unit = W * dilation
    Sp = -(-S // unit) * unit
    M = Sp // dilation
    nb = M // W

    def strided(t):
        t = jnp.pad(t, ((0, 0), (0, Sp - S), (0, 0), (0, 0)))
        return t.reshape(Bsz, M, dilation, H, d).transpose(0, 2, 1, 3, 4).reshape(Bsz, dilation, nb, W, H, d)

    def with_prev(t):
        prev = jnp.pad(t, ((0, 0), (0, 0), (1, 0), (0, 0), (0, 0), (0, 0)))[:, :, :-1]
        return jnp.concatenate([prev, t], axis=3)

    qs = strided(q)
    kk = with_prev(strided(k))
    vv = with_prev(strided(v))
    i = jnp.arange(W)
    j = jnp.arange(2 * W)
    steps = W + i[:, None] - j[None, :]
    first = jnp.arange(nb)[:, None, None] == 0
    valid = (steps >= 0) & (steps <= W) & ~(first & (j < W)[None, None, :])
    bias = -slopes[:, None, None] * (steps * dilation).astype(jnp.float32)
    s = jnp.einsum('brnqhd,brnkhd->brnhqk', qs, kk).astype(jnp.float32) * (d ** -0.5) + bias[None, None, None]
    s = jnp.where(valid[None, None, :, None], s, -jnp.inf)
    m = jnp.max(s, axis=-1, keepdims=True)
    p = jnp.exp(s - m)
    l = jnp.sum(p, axis=-1, keepdims=True)
    o = jnp.einsum('brnhqk,brnkhd->brnqhd', (p / l).astype(v.dtype), vv)
    lse = (m + jnp.log(l))[..., 0]
    o = o.reshape(Bsz, dilation, M, H, d).transpose(0, 2, 1, 3, 4).reshape(Bsz, Sp, H, d)[:, :S]
    lse = lse.transpose(0, 1, 2, 4, 3).reshape(Bsz, dilation, M, H).transpose(0, 2, 1, 3).reshape(Bsz, Sp, H)[:, :S]
    return o, lse


def dilated_attention(q, k, v, slopes):
    outs = []
    lses = []
    for window, dilation in DIL_PATTERNS:
        o, lse = dilated_branch(q, k, v, slopes, window, dilation)
        outs.append(o)
        lses.append(lse)
    wts = jax.nn.softmax(jnp.stack(lses, axis=0), axis=0)
    return jnp.einsum('pbsh,pbshd->bshd', wts.astype(q.dtype), jnp.stack(outs, axis=0))


def hybrid_mixer(h, layer, w_in, ret_norm, conv_w, conv_b, dt_bias, a_log, d_skip, ssd_norm_g, diff_lambda, diff_norm_g):
    Bsz, S, _ = h.shape
    split_points = np.cumsum(IN_SPLITS)[:-1].tolist()
    (r_q, r_k, r_v, r_g, s_z, s_xbc, s_dt, d_q, d_k, d_v, l_q, l_k, l_v) = jnp.split(h @ w_in, split_points, axis=-1)

    ret = retention(r_q.reshape(Bsz, S, RET_HEADS, RET_DK), r_k.reshape(Bsz, S, RET_HEADS, RET_DK),
                    r_v.reshape(Bsz, S, RET_HEADS, RET_DV), ret_norm)
    out_a = ret.reshape(Bsz, S, RET_HEADS * RET_DV) * jax.nn.silu(r_g)

    xbc = jax.nn.silu(causal_depthwise_conv(s_xbc, conv_w, conv_b))
    s_x, s_b, s_c = jnp.split(xbc, [SSD_INNER, SSD_INNER + SSD_GROUPS * SSD_STATE], axis=-1)
    s_x = s_x.reshape(Bsz, S, SSD_HEADS, SSD_HEADDIM)
    dt = jax.nn.softplus((s_dt + dt_bias).astype(jnp.float32))
    a = dt * (-jnp.exp(a_log.astype(jnp.float32)))
    y = ssd_chunked_scan(s_x * dt[..., None].astype(s_x.dtype), a,
                         s_b.reshape(Bsz, S, SSD_GROUPS, SSD_STATE), s_c.reshape(Bsz, S, SSD_GROUPS, SSD_STATE))
    y = y + s_x * d_skip[:, None].astype(s_x.dtype)
    y = (y.reshape(Bsz, S, SSD_INNER) * jax.nn.silu(s_z)).reshape(Bsz, S, SSD_GROUPS, SSD_INNER // SSD_GROUPS)
    out_b = rms_norm(y, ssd_norm_g.reshape(SSD_GROUPS, SSD_INNER // SSD_GROUPS)).reshape(Bsz, S, SSD_INNER)

    lam_init = 0.8 - 0.6 * math.exp(-0.3 * layer)
    lf = diff_lambda.astype(jnp.float32)
    lam = jnp.exp(jnp.sum(lf[0] * lf[1])) - jnp.exp(jnp.sum(lf[2] * lf[3])) + lam_init
    out_c = diff_attention(d_q.reshape(Bsz, S, DIFF_HEADS, 2, DIFF_HD), d_k.reshape(Bsz, S, DIFF_HEADS, 2, DIFF_HD),
                           d_v.reshape(Bsz, S, DIFF_HEADS, 2 * DIFF_HD), lam, diff_norm_g, lam_init,
                           alibi_slopes(DIFF_HEADS)).reshape(Bsz, S, 2 * DIFF_HEADS * DIFF_HD)

    out_d = dilated_attention(l_q.reshape(Bsz, S, DIL_HEADS, DIL_HD), l_k.reshape(Bsz, S, DIL_HEADS, DIL_HD),
                              l_v.reshape(Bsz, S, DIL_HEADS, DIL_HD), alibi_slopes(DIL_HEADS)).reshape(Bsz, S, DIL_HEADS * DIL_HD)

    return jnp.concatenate([out_a, out_b, out_c, out_d], axis=-1)


def setup_inputs(seed: int = 0) -> dict:
    key = jax.random.key(seed)
    ks = jax.random.split(key, 20)
    f32 = jnp.float32

    def nrm(k, shape, scale):
        return scale * jax.random.normal(k, shape, f32)

    def gain(k, shape):
        return 1.0 + 0.02 * jax.random.normal(k, shape, f32)

    x = jax.random.normal(ks[0], (BATCH, SEQ, D_MODEL), f32)
    norm_mix_pre = gain(ks[1], (DEPTH, D_MODEL))
    w_in = nrm(ks[2], (DEPTH, D_MODEL, IN_COLS), D_MODEL ** -0.5)
    ret_norm = gain(ks[3], (DEPTH, RET_HEADS * RET_DV))
    ssd_conv_w = nrm(ks[4], (DEPTH, SSD_CONV_DIM, SSD_CONV), SSD_CONV ** -0.5)
    ssd_conv_b = nrm(ks[5], (DEPTH, SSD_CONV_DIM), 0.02)
    dt0 = jnp.exp(jax.random.uniform(ks[6], (DEPTH, SSD_HEADS), f32, math.log(DT_MIN), math.log(DT_MAX)))
    ssd_dt_bias = dt0 + jnp.log(-jnp.expm1(-dt0))
    ssd_a_log = jnp.log(jax.random.uniform(ks[7], (DEPTH, SSD_HEADS), f32, 1.0, 16.0))
    ssd_d = gain(ks[8], (DEPTH, SSD_HEADS))
    ssd_norm = gain(ks[9], (DEPTH, SSD_INNER))
    diff_lambda = nrm(ks[10], (DEPTH, 4, DIFF_HD), 0.1)
    diff_norm = gain(ks[11], (DEPTH, 2 * DIFF_HD))
    w_out = nrm(ks[12], (DEPTH, MIX_WIDTH, D_MODEL), MIX_WIDTH ** -0.5)
    norm_mix_post = gain(ks[13], (DEPTH, D_MODEL))
    norm_ffn_pre = gain(ks[14], (DEPTH, D_MODEL))
    w_gate = nrm(ks[15], (DEPTH, D_MODEL, FFN_HIDDEN), D_MODEL ** -0.5)
    w_up = nrm(ks[16], (DEPTH, D_MODEL, FFN_HIDDEN), D_MODEL ** -0.5)
    w_down = nrm(ks[17], (DEPTH, FFN_HIDDEN, D_MODEL), FFN_HIDDEN ** -0.5)
    norm_ffn_post = gain(ks[18], (DEPTH, D_MODEL))
    return {'x': x, 'norm_mix_pre': norm_mix_pre, 'w_in': w_in, 'ret_norm': ret_norm,
            'ssd_conv_w': ssd_conv_w, 'ssd_conv_b': ssd_conv_b, 'ssd_dt_bias': ssd_dt_bias,
            'ssd_a_log': ssd_a_log, 'ssd_d': ssd_d, 'ssd_norm': ssd_norm, 'diff_lambda': diff_lambda,
            'diff_norm': diff_norm, 'w_out': w_out, 'norm_mix_post': norm_mix_post,
            'norm_ffn_pre': norm_ffn_pre, 'w_gate': w_gate, 'w_up': w_up, 'w_down': w_down,
            'norm_ffn_post': norm_ffn_post}


def reference(x, norm_mix_pre, w_in, ret_norm, ssd_conv_w, ssd_conv_b, ssd_dt_bias, ssd_a_log, ssd_d,
              ssd_norm, diff_lambda, diff_norm, w_out, norm_mix_post, norm_ffn_pre, w_gate, w_up, w_down,
              norm_ffn_post):
    for l in range(DEPTH):
        h = rms_norm(x, norm_mix_pre[l])
        mix = hybrid_mixer(h, l, w_in[l], ret_norm[l], ssd_conv_w[l], ssd_conv_b[l], ssd_dt_bias[l],
                           ssd_a_log[l], ssd_d[l], ssd_norm[l], diff_lambda[l], diff_norm[l])
        x = x + rms_norm(mix @ w_out[l], norm_mix_post[l])
        h = rms_norm(x, norm_ffn_pre[l])
        f = (jax.nn.silu(h @ w_gate[l]) * (h @ w_up[l])) @ w_down[l]
        x = x + rms_norm(f, norm_ffn_post[l])
    return x
```

```python
import functools
import math

import numpy as np
import jax
import jax.numpy as jnp
from jax import lax
from jax.experimental import pallas as pl
from jax.experimental.pallas import tpu as pltpu

F32 = jnp.float32
_MXU_DTYPE = jnp.bfloat16

D_MODEL = 2048
GROUP_WIDTH = D_MODEL // 4
NORM_EPS = 1e-6

RET_HEADS = 4
RET_DK = 64
RET_DV = 128
CHUNK = 128

SSD_HEADS = 8
SSD_HEADDIM = 64
SSD_GROUPS = 2
SSD_STATE = 128
SSD_CONV = 4
SSD_CONV_DIM = 1024

DIFF_HEADS = 4
DIFF_HD = 64
DIFF_BLOCK = 512

DIL_HEADS = 8
DIL_HD = 64
DIL_PATTERNS = ((128, 1), (512, 4), (2048, 16))

FFN_HIDDEN = 5632

QKV_COLS = 4096
_Q_RET, _K_RET, _V_RET = 0, 256, 512
_Q_DIFF, _K_DIFF, _V_DIFF = 1024, 1536, 2048
_Q_DIL, _K_DIL, _V_DIL = 2560, 3072, 3584
GATE_COLS = 2304
_G_RET, _Z_SSD, _XBC_SSD, _DT_SSD = 0, 512, 1024, 2048
LANES = 128

VMEM_LIMIT_BYTES = 56 * 1024 * 1024


def _params(semantics):
    return pltpu.CompilerParams(dimension_semantics=semantics, vmem_limit_bytes=VMEM_LIMIT_BYTES)


def _rms(x, gain):
    ms = jnp.mean(x * x, axis=-1, keepdims=True)
    return x * lax.rsqrt(ms + NORM_EPS) * gain


def _dot(a, b):
    return jnp.dot(a.astype(_MXU_DTYPE), b.astype(_MXU_DTYPE), preferred_element_type=F32)


def _dot_nt(a, b):
    return lax.dot_general(a.astype(_MXU_DTYPE), b.astype(_MXU_DTYPE), (((1,), (1,)), ((), ())),
                           preferred_element_type=F32)


def _silu(x):
    return x * jax.nn.sigmoid(x)


def _norm_matmul_kernel(x_ref, g_ref, w_ref, o_ref, h_ref):
    @pl.when(pl.program_id(1) == 0)
    def _():
        h_ref[...] = _rms(x_ref[...], g_ref[...]).astype(h_ref.dtype)

    o_ref[...] = jnp.dot(h_ref[...], w_ref[...], preferred_element_type=F32).astype(o_ref.dtype)


def _norm_matmul(x, gain, w, out_dtype, tm, tn):
    t, d = x.shape
    n = w.shape[1]
    return pl.pallas_call(
        _norm_matmul_kernel,
        grid=(t // tm, n // tn),
        in_specs=[pl.BlockSpec((tm, d), lambda i, j: (i, 0)),
                  pl.BlockSpec((1, d), lambda i, j: (0, 0)),
                  pl.BlockSpec((d, tn), lambda i, j: (0, j))],
        out_specs=pl.BlockSpec((tm, tn), lambda i, j: (i, j)),
        out_shape=jax.ShapeDtypeStruct((t, n), out_dtype),
        scratch_shapes=[pltpu.VMEM((tm, d), _MXU_DTYPE)],
        compiler_params=_params(("parallel", "arbitrary")),
        name="norm_matmul",
    )(x, gain, w)


def _retention_tables():
    h = np.arange(RET_HEADS, dtype=np.float32)
    log_gamma = np.log1p(-np.exp2(-5.0 - h)).astype(np.float32)
    pos = np.arange(CHUNK, dtype=np.float32)
    rel = pos[:, None] - pos[None, :]
    intra = np.where(rel >= 0, np.exp(log_gamma[:, None, None] * np.maximum(rel, 0.0)), 0.0)
    zeta = np.exp(log_gamma[:, None] * (CHUNK - 1.0 - pos)[None, :])
    xi = np.exp(log_gamma[:, None] * (pos + 1.0)[None, :])
    decay = np.exp(log_gamma * CHUNK)
    zeta_b = np.broadcast_to(zeta[:, :, None], (RET_HEADS, CHUNK, RET_DK))
    xi_b = np.broadcast_to(xi[:, :, None], (RET_HEADS, CHUNK, RET_DV))
    return (jnp.asarray(intra, F32), jnp.asarray(zeta_b, F32), jnp.asarray(xi_b, F32),
            [float(v) for v in decay.astype(np.float32)])


def _retention_kernel(q_ref, k_ref, v_ref, g_ref, intra_ref, zeta_ref, xi_ref, gain_ref, o_ref,
                      state_ref, *, chunks, decay):
    @pl.when(pl.program_id(1) == 0)
    def _():
        state_ref[...] = jnp.zeros_like(state_ref)

    for c in range(chunks):
        rows = pl.ds(c * CHUNK, CHUNK)
        for h in range(RET_HEADS):
            q = q_ref[rows, h * RET_DK:(h + 1) * RET_DK]
            k = k_ref[rows, h * RET_DK:(h + 1) * RET_DK]
            v = v_ref[rows, h * RET_DV:(h + 1) * RET_DV]
            state = state_ref[h]
            scores = _dot_nt(q, k) * intra_ref[h]
            y = _dot(scores, v) + _dot(q, state) * xi_ref[h]
            kz = k.astype(F32) * zeta_ref[h]
            state_ref[h] = state * decay[h] + _dot(kz.T, v)
            y = _rms(y, gain_ref[:, h * RET_DV:(h + 1) * RET_DV])
            o_ref[rows, h * RET_DV:(h + 1) * RET_DV] = (
                y * _silu(g_ref[rows, h * RET_DV:(h + 1) * RET_DV])).astype(o_ref.dtype)


def _retention(qkv, gates, ret_norm, batch, seq):
    rows = 512
    chunks = rows // CHUNK
    nblk = seq // rows
    intra, zeta_b, xi_b, decay = _retention_tables()
    row_map = lambda col: (lambda b, i: (b * nblk + i, col))
    const3 = lambda b, i: (0, 0, 0)
    return pl.pallas_call(
        functools.partial(_retention_kernel, chunks=chunks, decay=decay),
        grid=(batch, nblk),
        in_specs=[pl.BlockSpec((rows, 256), row_map(_Q_RET // 256)),
                  pl.BlockSpec((rows, 256), row_map(_K_RET // 256)),
                  pl.BlockSpec((rows, 512), row_map(_V_RET // 512)),
                  pl.BlockSpec((rows, 512), row_map(_G_RET // 512)),
                  pl.BlockSpec((RET_HEADS, CHUNK, CHUNK), const3),
                  pl.BlockSpec((RET_HEADS, CHUNK, RET_DK), const3),
                  pl.BlockSpec((RET_HEADS, CHUNK, RET_DV), const3),
                  pl.BlockSpec((1, GROUP_WIDTH), lambda b, i: (0, 0))],
        out_specs=pl.BlockSpec((rows, GROUP_WIDTH), row_map(0)),
        out_shape=jax.ShapeDtypeStruct((batch * seq, GROUP_WIDTH), _MXU_DTYPE),
        scratch_shapes=[pltpu.VMEM((RET_HEADS, RET_DK, RET_DV), F32)],
        compiler_params=_params(("parallel", "arbitrary")),
        name="retention",
    )(qkv, qkv, qkv, gates, intra, zeta_b, xi_b, ret_norm.reshape(1, GROUP_WIDTH))


_SSD_TAIL = 8


def _ssd_kernel(z_ref, xbc_ref, dt_ref, convw_ref, convb_ref, dtb_ref, alog_ref, dskip_ref, gain_ref,
                tril_ref, o_ref, ext_ref, hstate_ref, y_ref):
    @pl.when(pl.program_id(1) == 0)
    def _():
        ext_ref[0:_SSD_TAIL, :] = jnp.zeros((_SSD_TAIL, SSD_CONV_DIM), F32)
        hstate_ref[...] = jnp.zeros_like(hstate_ref)

    ext_ref[_SSD_TAIL:_SSD_TAIL + CHUNK, :] = xbc_ref[...]
    conv = convb_ref[...] + jnp.zeros((CHUNK, SSD_CONV_DIM), F32)
    for k in range(SSD_CONV):
        start = _SSD_TAIL - (SSD_CONV - 1) + k
        conv = conv + ext_ref[start:start + CHUNK, :] * convw_ref[k:k + 1, :]
    ext_ref[0:_SSD_TAIL, :] = ext_ref[CHUNK:CHUNK + _SSD_TAIL, :]
    xbc = _silu(conv)

    inner = SSD_HEADS * SSD_HEADDIM
    gs = SSD_GROUPS * SSD_STATE
    dt = jax.nn.softplus(dt_ref[...] + dtb_ref[...])
    a = dt * (-jnp.exp(alog_ref[...]))
    a_cum = jnp.dot(tril_ref[...], a, preferred_element_type=F32, precision=lax.Precision.HIGHEST)
    a_cum_t = a_cum.T
    row_i = lax.broadcasted_iota(jnp.int32, (CHUNK, CHUNK), 0)
    col_i = lax.broadcasted_iota(jnp.int32, (CHUNK, CHUNK), 1)
    causal = row_i >= col_i
    heads_per_group = SSD_HEADS // SSD_GROUPS

    for g in range(SSD_GROUPS):
        bm = xbc[:, inner + g * SSD_STATE:inner + (g + 1) * SSD_STATE]
        cm = xbc[:, inner + gs + g * SSD_STATE:inner + gs + (g + 1) * SSD_STATE]
        cb = _dot_nt(cm, bm)
        for r in range(heads_per_group):
            h = g * heads_per_group + r
            col = a_cum[:, h:h + 1]
            row = a_cum_t[h:h + 1, :]
            last = a_cum[CHUNK - 1:CHUNK, h:h + 1]
            seg = col - row
            decay_in = jnp.where(causal, jnp.exp(jnp.where(causal, seg, 0.0)), 0.0)
            x_h = xbc[:, h * SSD_HEADDIM:(h + 1) * SSD_HEADDIM]
            xdt = x_h * dt[:, h:h + 1]
            hprev = hstate_ref[h]
            y = _dot(cb * decay_in, xdt) + _dot_nt(cm, hprev) * jnp.exp(col)
            xw = xdt * jnp.exp(last - col)
            hstate_ref[h] = hprev * jnp.exp(last) + _dot(xw.T, bm)
            y_ref[:, h * SSD_HEADDIM:(h + 1) * SSD_HEADDIM] = y + x_h * dskip_ref[:, h:h + 1]

    y = y_ref[...] * _silu(z_ref[...])
    gw = inner // SSD_GROUPS
    for g in range(SSD_GROUPS):
        o_ref[:, g * gw:(g + 1) * gw] = _rms(y[:, g * gw:(g + 1) * gw],
                                             gain_ref[:, g * gw:(g + 1) * gw]).astype(o_ref.dtype)


def _pad_lanes(v):
    return jnp.pad(v.astype(F32), (0, LANES - v.shape[0])).reshape(1, LANES)


def _ssd(gates, conv_w, conv_b, dt_bias, a_log, d_skip, ssd_norm, batch, seq):
    nblk = seq // CHUNK
    row_map = lambda col: (lambda b, i: (b * nblk + i, col))
    const2 = lambda b, i: (0, 0)
    tril = jnp.asarray(np.tril(np.ones((CHUNK, CHUNK), np.float32)))
    return pl.pallas_call(
        _ssd_kernel,
        grid=(batch, nblk),
        in_specs=[pl.BlockSpec((CHUNK, 512), row_map(_Z_SSD // 512)),
                  pl.BlockSpec((CHUNK, SSD_CONV_DIM), row_map(_XBC_SSD // SSD_CONV_DIM)),
                  pl.BlockSpec((CHUNK, LANES), row_map(_DT_SSD // LANES)),
                  pl.BlockSpec((SSD_CONV, SSD_CONV_DIM), const2),
                  pl.BlockSpec((1, SSD_CONV_DIM), const2),
                  pl.BlockSpec((1, LANES), const2),
                  pl.BlockSpec((1, LANES), const2),
                  pl.BlockSpec((1, LANES), const2),
                  pl.BlockSpec((1, GROUP_WIDTH), const2),
                  pl.BlockSpec((CHUNK, CHUNK), const2)],
        out_specs=pl.BlockSpec((CHUNK, GROUP_WIDTH), row_map(0)),
        out_shape=jax.ShapeDtypeStruct((batch * seq, GROUP_WIDTH), _MXU_DTYPE),
        scratch_shapes=[pltpu.VMEM((_SSD_TAIL + CHUNK, SSD_CONV_DIM), F32),
                        pltpu.VMEM((SSD_HEADS, SSD_HEADDIM, SSD_STATE), F32),
                        pltpu.VMEM((CHUNK, GROUP_WIDTH), F32)],
        compiler_params=_params(("parallel", "arbitrary")),
        name="ssd",
    )(gates, gates, gates, conv_w.T.astype(F32), conv_b.reshape(1, SSD_CONV_DIM).astype(F32),
      _pad_lanes(dt_bias), _pad_lanes(a_log), _pad_lanes(d_skip), ssd_norm.reshape(1, GROUP_WIDTH), tril)


def _alibi_slopes(n):
    return np.exp2(-8.0 * (np.arange(n, dtype=np.float32) + 1.0) / n).astype(np.float32)


def _diff_tables(nq):
    slopes = _alibi_slopes(DIFF_HEADS)
    i = np.arange(DIFF_BLOCK, dtype=np.float32)
    dist = i[:, None] - i[None, :]
    off = -slopes[:, None, None] * dist[None]
    diag = np.where(dist[None] >= 0, off, -np.inf)
    bias = np.stack([off, diag], axis=1).astype(np.float32)
    blk = (-slopes[:, None] * (np.arange(nq, dtype=np.float32) * DIFF_BLOCK)[None, :]).astype(np.float32)
    return jnp.asarray(bias), jnp.asarray(blk)


def _diff_kernel(blk_ref, q_ref, k_ref, v_ref, bias_ref, lam_ref, gain_ref, o_ref,
                 qm_ref, m_ref, l_ref, acc_ref, *, lam_init):
    h = pl.program_id(1)
    qi = pl.program_id(2)
    ki = pl.program_id(3)
    tq, tk = DIFF_BLOCK, DIFF_BLOCK

    @pl.when(ki == 0)
    def _():
        q = q_ref[...]
        lane = lax.broadcasted_iota(jnp.int32, q.shape, 1)
        zero = jnp.zeros_like(q)
        qm_ref[0] = jnp.where(lane < DIFF_HD, q, zero)
        qm_ref[1] = jnp.where(lane >= DIFF_HD, q, zero)
        m_ref[...] = jnp.full(m_ref.shape, -jnp.inf, F32)
        l_ref[...] = jnp.zeros_like(l_ref)
        acc_ref[...] = jnp.zeros_like(acc_ref)

    @pl.when(ki <= qi)
    def _():
        k = k_ref[...]
        v = v_ref[...]
        bias = bias_ref[jnp.where(ki == qi, 1, 0)]
        blk = blk_ref[h, qi - ki]
        for c in range(2):
            s = _dot_nt(qm_ref[c], k) + bias
            m_prev = m_ref[c]
            m_next = jnp.maximum(m_prev, jnp.max(s, axis=-1, keepdims=True) + blk)
            p = jnp.exp(s - pltpu.repeat(m_next - blk, tk // LANES, axis=1))
            alpha = jnp.exp(m_prev - m_next)
            l_ref[c] = alpha * l_ref[c] + jnp.sum(p, axis=-1, keepdims=True)
            acc_ref[c] = alpha * acc_ref[c] + _dot(p, v)
            m_ref[c] = m_next

    @pl.when(ki == qi)
    def _():
        lf = lam_ref[...]
        lam = (jnp.exp(jnp.sum(lf[0:1] * lf[1:2], axis=-1, keepdims=True))
               - jnp.exp(jnp.sum(lf[2:3] * lf[3:4], axis=-1, keepdims=True)) + lam_init)
        o = acc_ref[0] / l_ref[0] - lam * (acc_ref[1] / l_ref[1])
        o_ref[...] = (_rms(o, gain_ref[...]) * (1.0 - lam_init)).astype(o_ref.dtype)


def _diff_attention(qkv, diff_lambda, diff_norm, lam_init, batch, seq):
    nq = seq // DIFF_BLOCK
    bias, blk = _diff_tables(nq)
    q_map = lambda col: (lambda b, h, qi, ki: (b * nq + qi, col + h))
    kv_map = lambda col: (lambda b, h, qi, ki: (b * nq + jnp.minimum(ki, qi), col + h))
    return pl.pallas_call(
        functools.partial(_diff_kernel, lam_init=lam_init),
        grid=(batch, DIFF_HEADS, nq, nq),
        in_specs=[pl.BlockSpec(memory_space=pltpu.SMEM),
                  pl.BlockSpec((DIFF_BLOCK, LANES), q_map(_Q_DIFF // LANES)),
                  pl.BlockSpec((DIFF_BLOCK, LANES), kv_map(_K_DIFF // LANES)),
                  pl.BlockSpec((DIFF_BLOCK, LANES), kv_map(_V_DIFF // LANES)),
                  pl.BlockSpec((None, 2, DIFF_BLOCK, DIFF_BLOCK), lambda b, h, qi, ki: (h, 0, 0, 0)),
                  pl.BlockSpec((4, DIFF_HD), lambda b, h, qi, ki: (0, 0)),
                  pl.BlockSpec((1, 2 * DIFF_HD), lambda b, h, qi, ki: (0, 0))],
        out_specs=pl.BlockSpec((DIFF_BLOCK, LANES), lambda b, h, qi, ki: (b * nq + qi, h)),
        out_shape=jax.ShapeDtypeStruct((batch * seq, GROUP_WIDTH), _MXU_DTYPE),
        scratch_shapes=[pltpu.VMEM((2, DIFF_BLOCK, LANES), _MXU_DTYPE),
                        pltpu.VMEM((2, DIFF_BLOCK, LANES), F32),
                        pltpu.VMEM((2, DIFF_BLOCK, LANES), F32),
                        pltpu.VMEM((2, DIFF_BLOCK, LANES), F32)],
        compiler_params=_params(("parallel", "parallel", "parallel", "arbitrary")),
        name="diff_attention",
    )(blk, qkv, qkv, qkv, bias, diff_lambda.astype(F32), diff_norm.reshape(1, 2 * DIFF_HD).astype(F32))


def _dilated_bias(dilation):
    slopes = _alibi_slopes(DIL_HEADS)
    w = CHUNK
    i = np.arange(w)
    j = np.arange(2 * w)
    steps = w + i[:, None] - j[None, :]
    valid = (steps >= 0) & (steps <= w)
    out = np.empty((DIL_HEADS // 2, 2, 2 * w, 2 * w), np.float32)
    for pair in range(DIL_HEADS // 2):
        for first in range(2):
            ok = valid & ~((first == 1) & (j < w)[None, :])
            for hh in range(2):
                b = -slopes[2 * pair + hh] * (steps * dilation).astype(np.float32)
                out[pair, first, hh * w:(hh + 1) * w, :] = np.where(ok, b, -np.inf)
    return jnp.asarray(out)


def _dilated_kernel(q_ref, kp_ref, kc_ref, vp_ref, vc_ref, bias_ref, o_ref, lse_ref):
    first = jnp.where(pl.program_id(2) == 0, 1, 0)
    w = CHUNK
    lane = lax.broadcasted_iota(jnp.int32, (w, LANES), 1)
    even = lane < DIL_HD
    for pair in range(DIL_HEADS // 2):
        lanes = slice(pair * LANES, (pair + 1) * LANES)
        q = q_ref[:, lanes]
        zero = jnp.zeros_like(q)
        qs = jnp.concatenate([jnp.where(even, q, zero), jnp.where(even, zero, q)], axis=0)
        kk = jnp.concatenate([kp_ref[:, lanes], kc_ref[:, lanes]], axis=0)
        vv = jnp.concatenate([vp_ref[:, lanes], vc_ref[:, lanes]], axis=0)
        s = _dot_nt(qs, kk) + bias_ref[pair, first]
        m = jnp.max(s, axis=-1, keepdims=True)
        p = jnp.exp(s - m)
        l = jnp.sum(p, axis=-1, keepdims=True)
        o2 = _dot(p / l, vv)
        lse = jnp.broadcast_to(m + jnp.log(l), (2 * w, LANES))
        o_ref[:, lanes] = jnp.where(even, o2[:w], o2[w:])
        lse_ref[:, lanes] = jnp.where(even, lse[:w], lse[w:])


def _dilated_branch(qkv, dilation, batch, seq):
    d = dilation
    nb = seq // (CHUNK * d)
    rows = batch * seq // d
    view = qkv.reshape(rows, d * QKV_COLS)
    cur = lambda col: (lambda b, r, n: (b * nb + n, r * (QKV_COLS // GROUP_WIDTH) + col))
    prev = lambda col: (lambda b, r, n: (b * nb + jnp.maximum(n - 1, 0), r * (QKV_COLS // GROUP_WIDTH) + col))
    blk = (CHUNK, GROUP_WIDTH)
    out_spec = pl.BlockSpec(blk, lambda b, r, n: (b * nb + n, r))
    o, lse = pl.pallas_call(
        _dilated_kernel,
        grid=(batch, d, nb),
        in_specs=[pl.BlockSpec(blk, cur(_Q_DIL // GROUP_WIDTH)),
                  pl.BlockSpec(blk, prev(_K_DIL // GROUP_WIDTH)),
                  pl.BlockSpec(blk, cur(_K_DIL // GROUP_WIDTH)),
                  pl.BlockSpec(blk, prev(_V_DIL // GROUP_WIDTH)),
                  pl.BlockSpec(blk, cur(_V_DIL // GROUP_WIDTH)),
                  pl.BlockSpec((DIL_HEADS // 2, 2, 2 * CHUNK, 2 * CHUNK), lambda b, r, n: (0, 0, 0, 0))],
        out_specs=[out_spec, out_spec],
        out_shape=[jax.ShapeDtypeStruct((rows, d * GROUP_WIDTH), F32)] * 2,
        compiler_params=_params(("parallel", "parallel", "arbitrary")),
        name=f"dilated_d{d}",
    )(view, view, view, view, view, _dilated_bias(d))
    return o.reshape(batch * seq, GROUP_WIDTH), lse.reshape(batch * seq, GROUP_WIDTH)


def _dilated_combine_kernel(o1, o2, o3, l1, l2, l3, out_ref):
    a, b, c = l1[...], l2[...], l3[...]
    mx = jnp.maximum(jnp.maximum(a, b), c)
    ea, eb, ec = jnp.exp(a - mx), jnp.exp(b - mx), jnp.exp(c - mx)
    out = (ea * o1[...] + eb * o2[...] + ec * o3[...]) / (ea + eb + ec)
    out_ref[...] = out.astype(out_ref.dtype)


def _dilated_attention(qkv, batch, seq):
    parts = [_dilated_branch(qkv, dilation, batch, seq) for _, dilation in DIL_PATTERNS]
    tm = 1024
    spec = pl.BlockSpec((tm, GROUP_WIDTH), lambda i: (i, 0))
    return pl.pallas_call(
        _dilated_combine_kernel,
        grid=(batch * seq // tm,),
        in_specs=[spec] * 6,
        out_specs=spec,
        out_shape=jax.ShapeDtypeStruct((batch * seq, GROUP_WIDTH), _MXU_DTYPE),
        compiler_params=_params(("parallel",)),
        name="dilated_combine",
    )(parts[0][0], parts[1][0], parts[2][0], parts[0][1], parts[1][1], parts[2][1])


def _outproj_kernel(a_ref, b_ref, c_ref, d_ref, w_ref, x_ref, g_ref, o_ref):
    gw = GROUP_WIDTH
    acc = jnp.dot(a_ref[...], w_ref[0:gw, :], preferred_element_type=F32)
    acc += jnp.dot(b_ref[...], w_ref[gw:2 * gw, :], preferred_element_type=F32)
    acc += jnp.dot(c_ref[...], w_ref[2 * gw:3 * gw, :], preferred_element_type=F32)
    acc += jnp.dot(d_ref[...], w_ref[3 * gw:4 * gw, :], preferred_element_type=F32)
    o_ref[...] = x_ref[...] + _rms(acc, g_ref[...])


def _outproj(parts, w_out, x, gain):
    t, d = x.shape
    tm = 512
    part_spec = pl.BlockSpec((tm, GROUP_WIDTH), lambda i: (i, 0))
    row_spec = pl.BlockSpec((tm, d), lambda i: (i, 0))
    return pl.pallas_call(
        _outproj_kernel,
        grid=(t // tm,),
        in_specs=[part_spec] * 4 + [pl.BlockSpec((d, d), lambda i: (0, 0)), row_spec,
                                    pl.BlockSpec((1, d), lambda i: (0, 0))],
        out_specs=row_spec,
        out_shape=jax.ShapeDtypeStruct((t, d), F32),
        compiler_params=_params(("parallel",)),
        name="outproj",
    )(*parts, w_out, x, gain)


def _ffn_kernel(x_ref, gpre_ref, wg_ref, wu_ref, wd_ref, gpost_ref, o_ref, h_ref, acc_ref):
    k = pl.program_id(1)

    @pl.when(k == 0)
    def _():
        h_ref[...] = _rms(x_ref[...], gpre_ref[...]).astype(h_ref.dtype)
        acc_ref[...] = jnp.zeros_like(acc_ref)

    h = h_ref[...]
    gate = jnp.dot(h, wg_ref[...], preferred_element_type=F32)
    up = jnp.dot(h, wu_ref[...], preferred_element_type=F32)
    acc_ref[...] += jnp.dot((_silu(gate) * up).astype(_MXU_DTYPE), wd_ref[...], preferred_element_type=F32)

    @pl.when(k == pl.num_programs(1) - 1)
    def _():
        o_ref[...] = x_ref[...] + _rms(acc_ref[...], gpost_ref[...])


def _ffn(x, gpre, w_gate, w_up, w_down, gpost):
    t, d = x.shape
    hidden = w_gate.shape[1]
    tm, tk = 512, 512
    row_spec = pl.BlockSpec((tm, d), lambda i, k: (i, 0))
    vec_spec = pl.BlockSpec((1, d), lambda i, k: (0, 0))
    return pl.pallas_call(
        _ffn_kernel,
        grid=(t // tm, hidden // tk),
        in_specs=[row_spec, vec_spec,
                  pl.BlockSpec((d, tk), lambda i, k: (0, k)),
                  pl.BlockSpec((d, tk), lambda i, k: (0, k)),
                  pl.BlockSpec((tk, d), lambda i, k: (k, 0)),
                  vec_spec],
        out_specs=row_spec,
        out_shape=jax.ShapeDtypeStruct((t, d), F32),
        scratch_shapes=[pltpu.VMEM((tm, d), _MXU_DTYPE), pltpu.VMEM((tm, d), F32)],
        compiler_params=_params(("parallel", "arbitrary")),
        name="ffn",
    )(x, gpre, w_gate, w_up, w_down, gpost)


def _split_w_in(w_in):
    sizes = (256, 256, 512, 512, 512, SSD_CONV_DIM, SSD_HEADS, 512, 512, 512, 512, 512, 512)
    offs = np.cumsum((0,) + sizes)
    (r_q, r_k, r_v, r_g, s_z, s_xbc, s_dt, d_q, d_k, d_v, l_q, l_k, l_v) = [
        w_in[:, offs[i]:offs[i + 1]] for i in range(len(sizes))]
    w_qkv = jnp.concatenate([r_q, r_k * RET_DK ** -0.5, r_v, d_q * DIFF_HD ** -0.5, d_k, d_v,
                             l_q * DIL_HD ** -0.5, l_k, l_v], axis=1)
    pad = jnp.zeros((w_in.shape[0], GATE_COLS - _DT_SSD - SSD_HEADS), w_in.dtype)
    w_gates = jnp.concatenate([r_g, s_z, s_xbc, s_dt, pad], axis=1)
    return w_qkv.astype(_MXU_DTYPE), w_gates.astype(_MXU_DTYPE)


def _mixer(x2, layer, gain_pre, w_in, ret_norm, conv_w, conv_b, dt_bias, a_log, d_skip, ssd_norm,
           diff_lambda, diff_norm, batch, seq):
    w_qkv, w_gates = _split_w_in(w_in)
    qkv = _norm_matmul(x2, gain_pre, w_qkv, _MXU_DTYPE, tm=1024, tn=1024)
    gates = _norm_matmul(x2, gain_pre, w_gates, F32, tm=1024, tn=768)
    out_a = _retention(qkv, gates, ret_norm, batch, seq)
    out_b = _ssd(gates, conv_w, conv_b, dt_bias, a_log, d_skip, ssd_norm, batch, seq)
    lam_init = 0.8 - 0.6 * math.exp(-0.3 * layer)
    out_c = _diff_attention(qkv, diff_lambda, diff_norm, lam_init, batch, seq)
    out_d = _dilated_attention(qkv, batch, seq)
    return out_a, out_b, out_c, out_d


def kernel(x, norm_mix_pre, w_in, ret_norm, ssd_conv_w, ssd_conv_b, ssd_dt_bias, ssd_a_log, ssd_d, ssd_norm, diff_lambda, diff_norm, w_out, norm_mix_post, norm_ffn_pre, w_gate, w_up, w_down, norm_ffn_post):
    batch, seq, d = x.shape
    assert d == D_MODEL and seq % max(w for w, _ in DIL_PATTERNS) == 0
    depth = w_in.shape[0]
    x2 = x.reshape(batch * seq, d)
    vec = lambda g: g.reshape(1, d).astype(F32)
    for l in range(depth):
        parts = _mixer(x2, l, vec(norm_mix_pre[l]), w_in[l], ret_norm[l], ssd_conv_w[l], ssd_conv_b[l],
                       ssd_dt_bias[l], ssd_a_log[l], ssd_d[l], ssd_norm[l], diff_lambda[l], diff_norm[l],
                       batch, seq)
        x2 = _outproj(parts, w_out[l].astype(_MXU_DTYPE), x2, vec(norm_mix_post[l]))
        x2 = _ffn(x2, vec(norm_ffn_pre[l]), w_gate[l].astype(_MXU_DTYPE), w_up[l].astype(_MXU_DTYPE),
                  w_down[l].astype(_MXU_DTYPE), vec(norm_ffn_post[l]))
    return x2.reshape(batch, seq, d)
```

```python
import functools
import math

import numpy as np
import jax
import jax.numpy as jnp
from jax import lax
from jax.experimental import pallas as pl
from jax.experimental.pallas import tpu as pltpu

F32 = jnp.float32
_MXU_DTYPE = jnp.bfloat16

D_MODEL = 2048
GROUP_WIDTH = D_MODEL // 4
NORM_EPS = 1e-6

RET_HEADS = 4
RET_DK = 64
RET_DV = 128
CHUNK = 128

SSD_HEADS = 8
SSD_HEADDIM = 64
SSD_GROUPS = 2
SSD_STATE = 128
SSD_CONV = 4
SSD_CONV_DIM = 1024

DIFF_HEADS = 4
DIFF_HD = 64
DIFF_BLOCK = 512
DIFF_ROWS = 64

DIL_HEADS = 8
DIL_HD = 64
DIL_DILATIONS = (1, 4, 16)
DIL_SPAN = CHUNK * max(DIL_DILATIONS)
DIL_TILE = 512

FFN_HIDDEN = 5632

QKV_COLS = 2560
_Q_RET, _K_RET, _V_RET = 0, 256, 512
_Q_DIFF, _K_DIFF, _V_DIFF = 1024, 1536, 2048
DIL_COLS = 3 * GROUP_WIDTH
GATE_COLS = 2304
_G_RET, _Z_SSD, _XBC_SSD, _DT_SSD = 0, 512, 1024, 2048
LANES = 128

VMEM_LIMIT_BYTES = 56 * 1024 * 1024


def _params(semantics):
    return pltpu.CompilerParams(dimension_semantics=semantics, vmem_limit_bytes=VMEM_LIMIT_BYTES)


def _rms(x, gain):
    ms = jnp.mean(x * x, axis=-1, keepdims=True)
    return x * lax.rsqrt(ms + NORM_EPS) * gain


def _dot(a, b):
    return jnp.dot(a.astype(_MXU_DTYPE), b.astype(_MXU_DTYPE), preferred_element_type=F32)


def _dot_nt(a, b):
    return lax.dot_general(a.astype(_MXU_DTYPE), b.astype(_MXU_DTYPE), (((1,), (1,)), ((), ())),
                           preferred_element_type=F32)


def _silu(x):
    return x * jax.nn.sigmoid(x)


def _norm_matmul_kernel(x_ref, g_ref, w_ref, o_ref):
    h = _rms(x_ref[...], g_ref[...]).astype(_MXU_DTYPE)
    o_ref[...] = jnp.dot(h, w_ref[...], preferred_element_type=F32).astype(o_ref.dtype)


def _norm_matmul(x, gain, w, out_dtype):
    t, d = x.shape
    n = w.shape[1]
    tm = 512
    return pl.pallas_call(
        _norm_matmul_kernel,
        grid=(t // tm,),
        in_specs=[pl.BlockSpec((tm, d), lambda i: (i, 0)),
                  pl.BlockSpec((1, d), lambda i: (0, 0)),
                  pl.BlockSpec((d, n), lambda i: (0, 0))],
        out_specs=pl.BlockSpec((tm, n), lambda i: (i, 0)),
        out_shape=jax.ShapeDtypeStruct((t, n), out_dtype),
        compiler_params=_params(("parallel",)),
        name="norm_matmul",
    )(x, gain, w)


def _dil_proj_kernel(x_ref, g_ref, w_ref, nat_ref, *rest):
    strided_refs, res_ref = rest[:-1], rest[-1]
    h = _rms(x_ref[...], g_ref[...]).astype(_MXU_DTYPE)
    res = jnp.dot(h, w_ref[...], preferred_element_type=F32)
    nat_ref[...] = res.astype(nat_ref.dtype)
    nblk = res.shape[1] // LANES
    for c in range(nblk):
        res_ref[c] = res[:, c * LANES:(c + 1) * LANES]
    for d, out_ref in zip(DIL_DILATIONS[1:], strided_refs):
        n = DIL_TILE // d
        for r in range(d):
            for c in range(nblk):
                out_ref[r * n:(r + 1) * n, c * LANES:(c + 1) * LANES] = (
                    res_ref[c, pl.ds(r, n, stride=d), :].astype(out_ref.dtype))


def _dil_proj(x, gain, w):
    t, d = x.shape
    n = w.shape[1]
    tm = DIL_TILE
    n_out = len(DIL_DILATIONS)
    out_spec = pl.BlockSpec((tm, n), lambda i: (i, 0))
    return pl.pallas_call(
        _dil_proj_kernel,
        grid=(t // tm,),
        in_specs=[pl.BlockSpec((tm, d), lambda i: (i, 0)),
                  pl.BlockSpec((1, d), lambda i: (0, 0)),
                  pl.BlockSpec((d, n), lambda i: (0, 0))],
        out_specs=[out_spec] * n_out,
        out_shape=[jax.ShapeDtypeStruct((t, n), _MXU_DTYPE)] * n_out,
        scratch_shapes=[pltpu.VMEM((n // LANES, tm, LANES), F32)],
        compiler_params=_params(("parallel",)),
        name="dil_proj",
    )(x, gain, w)


def _retention_tables():
    h = np.arange(RET_HEADS, dtype=np.float32)
    log_gamma = np.log1p(-np.exp2(-5.0 - h)).astype(np.float32)
    pos = np.arange(CHUNK, dtype=np.float32)
    rel = pos[:, None] - pos[None, :]
    intra = np.where(rel >= 0, np.exp(log_gamma[:, None, None] * np.maximum(rel, 0.0)), 0.0)
    zeta = np.exp(log_gamma[:, None] * (CHUNK - 1.0 - pos)[None, :])
    xi = np.exp(log_gamma[:, None] * (pos + 1.0)[None, :])
    decay = np.exp(log_gamma * CHUNK)
    zeta_b = np.broadcast_to(zeta[:, :, None], (RET_HEADS, CHUNK, RET_DK))
    xi_b = np.broadcast_to(xi[:, :, None], (RET_HEADS, CHUNK, RET_DV))
    return (jnp.asarray(intra, F32), jnp.asarray(zeta_b, F32), jnp.asarray(xi_b, F32),
            [float(v) for v in decay.astype(np.float32)])


def _retention_kernel(q_ref, k_ref, v_ref, g_ref, intra_ref, zeta_ref, xi_ref, gain_ref, o_ref,
                      state_ref, *, chunks, decay):
    @pl.when(pl.program_id(1) == 0)
    def _():
        state_ref[...] = jnp.zeros_like(state_ref)

    for c in range(chunks):
        rows = pl.ds(c * CHUNK, CHUNK)
        for h in range(RET_HEADS):
            q = q_ref[rows, h * RET_DK:(h + 1) * RET_DK]
            k = k_ref[rows, h * RET_DK:(h + 1) * RET_DK]
            v = v_ref[rows, h * RET_DV:(h + 1) * RET_DV]
            state = state_ref[h]
            scores = _dot_nt(q, k) * intra_ref[h]
            y = _dot(scores, v) + _dot(q, state) * xi_ref[h]
            kz = k.astype(F32) * zeta_ref[h]
            state_ref[h] = state * decay[h] + _dot(kz.T, v)
            y = _rms(y, gain_ref[:, h * RET_DV:(h + 1) * RET_DV])
            o_ref[rows, h * RET_DV:(h + 1) * RET_DV] = (
                y * _silu(g_ref[rows, h * RET_DV:(h + 1) * RET_DV])).astype(o_ref.dtype)


def _retention(qkv, gates, ret_norm, batch, seq):
    rows = 512
    chunks = rows // CHUNK
    nblk = seq // rows
    intra, zeta_b, xi_b, decay = _retention_tables()
    row_map = lambda col: (lambda b, i: (b * nblk + i, col))
    const3 = lambda b, i: (0, 0, 0)
    return pl.pallas_call(
        functools.partial(_retention_kernel, chunks=chunks, decay=decay),
        grid=(batch, nblk),
        in_specs=[pl.BlockSpec((rows, 256), row_map(_Q_RET // 256)),
                  pl.BlockSpec((rows, 256), row_map(_K_RET // 256)),
                  pl.BlockSpec((rows, 512), row_map(_V_RET // 512)),
                  pl.BlockSpec((rows, 512), row_map(_G_RET // 512)),
                  pl.BlockSpec((RET_HEADS, CHUNK, CHUNK), const3),
                  pl.BlockSpec((RET_HEADS, CHUNK, RET_DK), const3),
                  pl.BlockSpec((RET_HEADS, CHUNK, RET_DV), const3),
                  pl.BlockSpec((1, GROUP_WIDTH), lambda b, i: (0, 0))],
        out_specs=pl.BlockSpec((rows, GROUP_WIDTH), row_map(0)),
        out_shape=jax.ShapeDtypeStruct((batch * seq, GROUP_WIDTH), _MXU_DTYPE),
        scratch_shapes=[pltpu.VMEM((RET_HEADS, RET_DK, RET_DV), F32)],
        compiler_params=_params(("parallel", "arbitrary")),
        name="retention",
    )(qkv, qkv, qkv, gates, intra, zeta_b, xi_b, ret_norm.reshape(1, GROUP_WIDTH))


_SSD_TAIL = 8


def _ssd_kernel(z_ref, xbc_ref, dt_ref, convw_ref, convb_ref, dtb_ref, alog_ref, dskip_ref, gain_ref,
                tril_ref, o_ref, ext_ref, hstate_ref, y_ref):
    @pl.when(pl.program_id(1) == 0)
    def _():
        ext_ref[0:_SSD_TAIL, :] = jnp.zeros((_SSD_TAIL, SSD_CONV_DIM), F32)
        hstate_ref[...] = jnp.zeros_like(hstate_ref)

    ext_ref[_SSD_TAIL:_SSD_TAIL + CHUNK, :] = xbc_ref[...]
    conv = convb_ref[...] + jnp.zeros((CHUNK, SSD_CONV_DIM), F32)
    for k in range(SSD_CONV):
        start = _SSD_TAIL - (SSD_CONV - 1) + k
        conv = conv + ext_ref[start:start + CHUNK, :] * convw_ref[k:k + 1, :]
    ext_ref[0:_SSD_TAIL, :] = ext_ref[CHUNK:CHUNK + _SSD_TAIL, :]
    xbc = _silu(conv)

    inner = SSD_HEADS * SSD_HEADDIM
    gs = SSD_GROUPS * SSD_STATE
    dt = jax.nn.softplus(dt_ref[...] + dtb_ref[...])
    a = dt * (-jnp.exp(alog_ref[...]))
    a_cum = jnp.dot(tril_ref[...], a, preferred_element_type=F32, precision=lax.Precision.HIGHEST)
    a_cum_t = a_cum.T
    row_i = lax.broadcasted_iota(jnp.int32, (CHUNK, CHUNK), 0)
    col_i = lax.broadcasted_iota(jnp.int32, (CHUNK, CHUNK), 1)
    causal = row_i >= col_i
    heads_per_group = SSD_HEADS // SSD_GROUPS

    for g in range(SSD_GROUPS):
        bm = xbc[:, inner + g * SSD_STATE:inner + (g + 1) * SSD_STATE]
        cm = xbc[:, inner + gs + g * SSD_STATE:inner + gs + (g + 1) * SSD_STATE]
        cb = _dot_nt(cm, bm)
        for r in range(heads_per_group):
            h = g * heads_per_group + r
            col = a_cum[:, h:h + 1]
            row = a_cum_t[h:h + 1, :]
            last = a_cum[CHUNK - 1:CHUNK, h:h + 1]
            seg = col - row
            decay_in = jnp.where(causal, jnp.exp(jnp.where(causal, seg, 0.0)), 0.0)
            x_h = xbc[:, h * SSD_HEADDIM:(h + 1) * SSD_HEADDIM]
            xdt = x_h * dt[:, h:h + 1]
            hprev = hstate_ref[h]
            y = _dot(cb * decay_in, xdt) + _dot_nt(cm, hprev) * jnp.exp(col)
            xw = xdt * jnp.exp(last - col)
            hstate_ref[h] = hprev * jnp.exp(last) + _dot(xw.T, bm)
            y_ref[:, h * SSD_HEADDIM:(h + 1) * SSD_HEADDIM] = y + x_h * dskip_ref[:, h:h + 1]

    y = y_ref[...] * _silu(z_ref[...])
    gw = inner // SSD_GROUPS
    for g in range(SSD_GROUPS):
        o_ref[:, g * gw:(g + 1) * gw] = _rms(y[:, g * gw:(g + 1) * gw],
                                             gain_ref[:, g * gw:(g + 1) * gw]).astype(o_ref.dtype)


def _pad_lanes(v):
    return jnp.pad(v.astype(F32), (0, LANES - v.shape[0])).reshape(1, LANES)


def _ssd(gates, conv_w, conv_b, dt_bias, a_log, d_skip, ssd_norm, batch, seq):
    nblk = seq // CHUNK
    row_map = lambda col: (lambda b, i: (b * nblk + i, col))
    const2 = lambda b, i: (0, 0)
    tril = jnp.asarray(np.tril(np.ones((CHUNK, CHUNK), np.float32)))
    return pl.pallas_call(
        _ssd_kernel,
        grid=(batch, nblk),
        in_specs=[pl.BlockSpec((CHUNK, 512), row_map(_Z_SSD // 512)),
                  pl.BlockSpec((CHUNK, SSD_CONV_DIM), row_map(_XBC_SSD // SSD_CONV_DIM)),
                  pl.BlockSpec((CHUNK, LANES), row_map(_DT_SSD // LANES)),
                  pl.BlockSpec((SSD_CONV, SSD_CONV_DIM), const2),
                  pl.BlockSpec((1, SSD_CONV_DIM), const2),
                  pl.BlockSpec((1, LANES), const2),
                  pl.BlockSpec((1, LANES), const2),
                  pl.BlockSpec((1, LANES), const2),
                  pl.BlockSpec((1, GROUP_WIDTH), const2),
                  pl.BlockSpec((CHUNK, CHUNK), const2)],
        out_specs=pl.BlockSpec((CHUNK, GROUP_WIDTH), row_map(0)),
        out_shape=jax.ShapeDtypeStruct((batch * seq, GROUP_WIDTH), _MXU_DTYPE),
        scratch_shapes=[pltpu.VMEM((_SSD_TAIL + CHUNK, SSD_CONV_DIM), F32),
                        pltpu.VMEM((SSD_HEADS, SSD_HEADDIM, SSD_STATE), F32),
                        pltpu.VMEM((CHUNK, GROUP_WIDTH), F32)],
        compiler_params=_params(("parallel", "arbitrary")),
        name="ssd",
    )(gates, gates, gates, conv_w.T.astype(F32), conv_b.reshape(1, SSD_CONV_DIM).astype(F32),
      _pad_lanes(dt_bias), _pad_lanes(a_log), _pad_lanes(d_skip), ssd_norm.reshape(1, GROUP_WIDTH), tril)


def _alibi_slopes(n):
    return np.exp2(-8.0 * (np.arange(n, dtype=np.float32) + 1.0) / n).astype(np.float32)


_ALIBI_SPLIT = 32


def _alibi_features(rows, first_row, lane0, slope, key_side):
    pos = (lax.broadcasted_iota(jnp.int32, (rows, LANES), 0) + first_row) % DIFF_BLOCK
    hi = (pos // _ALIBI_SPLIT * _ALIBI_SPLIT).astype(F32) * slope
    lo = (pos % _ALIBI_SPLIT).astype(F32) * slope
    lane = lax.broadcasted_iota(jnp.int32, (rows, LANES), 1) - lane0
    one = jnp.ones((rows, LANES), F32)
    if key_side:
        feat = jnp.where(lane == 2, hi, jnp.where(lane == 3, lo, one))
    else:
        feat = jnp.where(lane == 0, -hi, jnp.where(lane == 1, -lo, one))
    return feat, (lane >= 0) & (lane < 4)


def _diff_kernel(slope_ref, blk_ref, q_ref, k_ref, v_ref, lam_ref, gain_ref, o_ref,
                 qm_ref, km_ref, m_ref, l_ref, acc_ref, *, lam_init, seq):
    h = pl.program_id(1)
    qi = pl.program_id(2)
    tq = tk = DIFF_BLOCK
    slope = slope_ref[h]
    feat_lane = (DIFF_HD, 0)

    @pl.when(qi == 0)
    def _():
        for j in range(seq // tk):
            rows = slice(j * tk, (j + 1) * tk)
            k = k_ref[rows, :]
            for c in range(2):
                feat, is_feat = _alibi_features(tk, 0, feat_lane[c], slope, key_side=True)
                km_ref[c, rows, :] = jnp.where(is_feat, feat.astype(k.dtype), k)

    q = q_ref[...]
    lane = lax.broadcasted_iota(jnp.int32, q.shape, 1)
    for c in range(2):
        feat, is_feat = _alibi_features(tq, 0, feat_lane[c], slope, key_side=False)
        own = (lane >= c * DIFF_HD) & (lane < (c + 1) * DIFF_HD)
        qm_ref[c] = jnp.where(own, q, jnp.where(is_feat, feat, 0.0).astype(q.dtype))
    m_ref[...] = jnp.full(m_ref.shape, -jnp.inf, F32)
    l_ref[...] = jnp.zeros_like(l_ref)
    acc_ref[...] = jnp.zeros_like(acc_ref)

    def block(kj, masked):
        start = pl.multiple_of(kj * tk, tk)
        v = v_ref[pl.ds(start, tk), :]
        blk = blk_ref[h, qi - kj]
        for c in range(2):
            s = _dot_nt(qm_ref[c], km_ref[c, pl.ds(start, tk), :])
            probs, alphas = [], []
            for r in range(tq // DIFF_ROWS):
                rows = slice(r * DIFF_ROWS, (r + 1) * DIFF_ROWS)
                sr = s[rows]
                if masked:
                    row_i = lax.broadcasted_iota(jnp.int32, sr.shape, 0) + r * DIFF_ROWS
                    col_i = lax.broadcasted_iota(jnp.int32, sr.shape, 1)
                    sr = jnp.where(row_i >= col_i, sr, -jnp.inf)
                m_prev = m_ref[c, rows]
                m_next = jnp.maximum(m_prev, jnp.max(sr, axis=-1, keepdims=True) + blk)
                p = jnp.exp(sr - pltpu.repeat(m_next - blk, tk // LANES, axis=1))
                alpha = jnp.exp(m_prev - m_next)
                l_ref[c, rows] = alpha * l_ref[c, rows] + jnp.sum(p, axis=-1, keepdims=True)
                m_ref[c, rows] = m_next
                probs.append(p.astype(_MXU_DTYPE))
                alphas.append(alpha)
            acc_ref[c] = jnp.concatenate(alphas, axis=0) * acc_ref[c] + _dot(jnp.concatenate(probs, axis=0), v)

    def body(kj, carry):
        block(kj, masked=False)
        return carry

    lax.fori_loop(0, qi, body, 0)
    block(qi, masked=True)

    lf = lam_ref[...]
    lam = (jnp.exp(jnp.sum(lf[0:1] * lf[1:2], axis=-1, keepdims=True))
           - jnp.exp(jnp.sum(lf[2:3] * lf[3:4], axis=-1, keepdims=True)) + lam_init)
    o = acc_ref[0] / l_ref[0] - lam * (acc_ref[1] / l_ref[1])
    o_ref[...] = (_rms(o, gain_ref[...]) * (1.0 - lam_init)).astype(o_ref.dtype)


def _diff_attention(qkv, diff_lambda, diff_norm, lam_init, batch, seq):
    nq = seq // DIFF_BLOCK
    slopes = _alibi_slopes(DIFF_HEADS)
    blk = (-slopes[:, None] * (np.arange(nq, dtype=np.float32) * DIFF_BLOCK)[None, :]).astype(np.float32)
    seq_spec = lambda col: pl.BlockSpec((seq, LANES), lambda b, h, qi: (b, col + h))
    const2 = lambda b, h, qi: (0, 0)
    return pl.pallas_call(
        functools.partial(_diff_kernel, lam_init=lam_init, seq=seq),
        grid=(batch, DIFF_HEADS, nq),
        in_specs=[pl.BlockSpec(memory_space=pltpu.SMEM),
                  pl.BlockSpec(memory_space=pltpu.SMEM),
                  pl.BlockSpec((DIFF_BLOCK, LANES), lambda b, h, qi: (b * nq + qi, _Q_DIFF // LANES + h)),
                  seq_spec(_K_DIFF // LANES),
                  seq_spec(_V_DIFF // LANES),
                  pl.BlockSpec((4, DIFF_HD), const2),
                  pl.BlockSpec((1, 2 * DIFF_HD), const2)],
        out_specs=pl.BlockSpec((DIFF_BLOCK, LANES), lambda b, h, qi: (b * nq + qi, h)),
        out_shape=jax.ShapeDtypeStruct((batch * seq, GROUP_WIDTH), _MXU_DTYPE),
        scratch_shapes=[pltpu.VMEM((2, DIFF_BLOCK, LANES), _MXU_DTYPE),
                        pltpu.VMEM((2, seq, LANES), _MXU_DTYPE),
                        pltpu.VMEM((2, DIFF_BLOCK, LANES), F32),
                        pltpu.VMEM((2, DIFF_BLOCK, LANES), F32),
                        pltpu.VMEM((2, DIFF_BLOCK, LANES), F32)],
        compiler_params=_params(("parallel", "parallel", "arbitrary")),
        name="diff_attention",
    )(jnp.asarray(slopes), jnp.asarray(blk), qkv, qkv, qkv, diff_lambda.astype(F32),
      diff_norm.reshape(1, 2 * DIFF_HD).astype(F32))


def _dilated_bias(dilation):
    slopes = _alibi_slopes(DIL_HEADS)
    w = CHUNK
    i = np.arange(w)
    j = np.arange(2 * w)
    steps = w + i[:, None] - j[None, :]
    valid = (steps >= 0) & (steps <= w)
    out = np.empty((DIL_HEADS // 2, 2, 2 * w, 2 * w), np.float32)
    for pair in range(DIL_HEADS // 2):
        for first in range(2):
            ok = valid & ~((first == 1) & (j < w)[None, :])
            for hh in range(2):
                b = -slopes[2 * pair + hh] * (steps * dilation).astype(np.float32)
                out[pair, first, hh * w:(hh + 1) * w, :] = np.where(ok, b, -np.inf)
    return jnp.asarray(out)


def _dilated_rows(ref, dilation, residue, block):
    if dilation == 1:
        return ref[block * CHUNK:(block + 1) * CHUNK, :]
    per = DIL_TILE // dilation
    tiles_per_block = CHUNK // per
    pieces = []
    for t in range(block * tiles_per_block, (block + 1) * tiles_per_block):
        pieces.append(ref[t * DIL_TILE + residue * per:t * DIL_TILE + (residue + 1) * per, :])
    return pieces[0] if len(pieces) == 1 else jnp.concatenate(pieces, axis=0)


def _dilated_kernel(q_ref, kp_ref, kc_ref, vp_ref, vc_ref, bias_ref, o_ref, lse_ref, *, dilation):
    d = dilation
    w = CHUNK
    first_step = pl.program_id(1) == 0
    lane = lax.broadcasted_iota(jnp.int32, (w, LANES), 1)
    even = lane < DIL_HD
    blocks_per_residue = DIL_SPAN // (w * d)
    for tile in range(DIL_SPAN // w):
        r, n = tile // blocks_per_residue, tile % blocks_per_residue
        q_t = _dilated_rows(q_ref, d, r, n)
        kc_t = _dilated_rows(kc_ref, d, r, n)
        vc_t = _dilated_rows(vc_ref, d, r, n)
        if n > 0:
            kp_t = _dilated_rows(kc_ref, d, r, n - 1)
            vp_t = _dilated_rows(vc_ref, d, r, n - 1)
            first = 0
        else:
            kp_t = _dilated_rows(kp_ref, d, r, 0)
            vp_t = _dilated_rows(vp_ref, d, r, 0)
            first = jnp.where(first_step, 1, 0)
        rows = pl.ds(tile * w, w) if d == 1 else pl.ds(n * w * d + r, w, stride=d)
        for pair in range(DIL_HEADS // 2):
            lanes = slice(pair * LANES, (pair + 1) * LANES)
            q = q_t[:, lanes]
            zero = jnp.zeros_like(q)
            qs = jnp.concatenate([jnp.where(even, q, zero), jnp.where(even, zero, q)], axis=0)
            kk = jnp.concatenate([kp_t[:, lanes], kc_t[:, lanes]], axis=0)
            vv = jnp.concatenate([vp_t[:, lanes], vc_t[:, lanes]], axis=0)
            s = _dot_nt(qs, kk) + bias_ref[pair, first]
            m = jnp.max(s, axis=-1, keepdims=True)
            p = jnp.exp(s - m)
            l = jnp.sum(p, axis=-1, keepdims=True)
            o2 = _dot(p / l, vv)
            lse = jnp.broadcast_to(m + jnp.log(l), (2 * w, LANES))
            o_ref[pair, rows, :] = jnp.where(even, o2[:w], o2[w:])
            lse_ref[pair, rows, :] = jnp.where(even, lse[:w], lse[w:])


def _dilated_branch(dil, dilation, batch, seq):
    d = dilation
    nstep = seq // DIL_SPAN
    prev_rows = CHUNK * d if d > 1 else CHUNK
    ratio = DIL_SPAN // prev_rows
    cur = lambda col: pl.BlockSpec((DIL_SPAN, GROUP_WIDTH), lambda b, i: (b * nstep + i, col))
    prev = lambda col: pl.BlockSpec(
        (prev_rows, GROUP_WIDTH), lambda b, i: (jnp.maximum((b * nstep + i) * ratio - 1, 0), col))
    pairs = DIL_HEADS // 2
    out_spec = pl.BlockSpec((pairs, DIL_SPAN, LANES), lambda b, i: (0, b * nstep + i, 0))
    return pl.pallas_call(
        functools.partial(_dilated_kernel, dilation=d),
        grid=(batch, nstep),
        in_specs=[cur(0), prev(1), cur(1), prev(2), cur(2),
                  pl.BlockSpec((DIL_HEADS // 2, 2, 2 * CHUNK, 2 * CHUNK), lambda b, i: (0, 0, 0, 0))],
        out_specs=[out_spec, out_spec],
        out_shape=[jax.ShapeDtypeStruct((pairs, batch * seq, LANES), F32)] * 2,
        compiler_params=_params(("parallel", "arbitrary")),
        name=f"dilated_d{d}",
    )(dil, dil, dil, dil, dil, _dilated_bias(d))


def _dilated_combine_kernel(o1, o2, o3, l1, l2, l3, out_ref):
    a, b, c = l1[...], l2[...], l3[...]
    mx = jnp.maximum(jnp.maximum(a, b), c)
    ea, eb, ec = jnp.exp(a - mx), jnp.exp(b - mx), jnp.exp(c - mx)
    out = (ea * o1[...] + eb * o2[...] + ec * o3[...]) / (ea + eb + ec)
    for pair in range(out.shape[0]):
        out_ref[:, pair * LANES:(pair + 1) * LANES] = out[pair].astype(out_ref.dtype)


def _dilated_attention(dils, batch, seq):
    parts = [_dilated_branch(dil, d, batch, seq) for dil, d in zip(dils, DIL_DILATIONS)]
    tm = 1024
    spec = pl.BlockSpec((DIL_HEADS // 2, tm, LANES), lambda i: (0, i, 0))
    return pl.pallas_call(
        _dilated_combine_kernel,
        grid=(batch * seq // tm,),
        in_specs=[spec] * 6,
        out_specs=pl.BlockSpec((tm, GROUP_WIDTH), lambda i: (i, 0)),
        out_shape=jax.ShapeDtypeStruct((batch * seq, GROUP_WIDTH), _MXU_DTYPE),
        compiler_params=_params(("parallel",)),
        name="dilated_combine",
    )(parts[0][0], parts[1][0], parts[2][0], parts[0][1], parts[1][1], parts[2][1])


def _outproj_kernel(a_ref, b_ref, c_ref, d_ref, w_ref, x_ref, g_ref, o_ref):
    gw = GROUP_WIDTH
    acc = jnp.dot(a_ref[...], w_ref[0:gw, :], preferred_element_type=F32)
    acc += jnp.dot(b_ref[...], w_ref[gw:2 * gw, :], preferred_element_type=F32)
    acc += jnp.dot(c_ref[...], w_ref[2 * gw:3 * gw, :], preferred_element_type=F32)
    acc += jnp.dot(d_ref[...], w_ref[3 * gw:4 * gw, :], preferred_element_type=F32)
    o_ref[...] = x_ref[...] + _rms(acc, g_ref[...])


def _outproj(parts, w_out, x, gain):
    t, d = x.shape
    tm = 512
    part_spec = pl.BlockSpec((tm, GROUP_WIDTH), lambda i: (i, 0))
    row_spec = pl.BlockSpec((tm, d), lambda i: (i, 0))
    return pl.pallas_call(
        _outproj_kernel,
        grid=(t // tm,),
        in_specs=[part_spec] * 4 + [pl.BlockSpec((d, d), lambda i: (0, 0)), row_spec,
                                    pl.BlockSpec((1, d), lambda i: (0, 0))],
        out_specs=row_spec,
        out_shape=jax.ShapeDtypeStruct((t, d), F32),
        compiler_params=_params(("parallel",)),
        name="outproj",
    )(*parts, w_out, x, gain)


def _ffn_kernel(x_ref, gpre_ref, wg_ref, wu_ref, wd_ref, gpost_ref, o_ref, h_ref):
    k = pl.program_id(1)

    @pl.when(k == 0)
    def _():
        h_ref[...] = _rms(x_ref[...], gpre_ref[...]).astype(h_ref.dtype)

    h = h_ref[...]
    gate = jnp.dot(h, wg_ref[...], preferred_element_type=F32)
    up = jnp.dot(h, wu_ref[...], preferred_element_type=F32)
    down = jnp.dot((_silu(gate) * up).astype(_MXU_DTYPE), wd_ref[...], preferred_element_type=F32)

    @pl.when(k == 0)
    def _():
        o_ref[...] = down

    @pl.when(k > 0)
    def _():
        o_ref[...] += down

    @pl.when(k == pl.num_programs(1) - 1)
    def _():
        o_ref[...] = x_ref[...] + _rms(o_ref[...], gpost_ref[...])


def _ffn(x, gpre, w_gate, w_up, w_down, gpost):
    t, d = x.shape
    hidden = w_gate.shape[1]
    tm, tk = 1024, 512
    row_spec = pl.BlockSpec((tm, d), lambda i, k: (i, 0))
    vec_spec = pl.BlockSpec((1, d), lambda i, k: (0, 0))
    x_spec = pl.BlockSpec((tm, d), lambda i, k: (i, 0), pipeline_mode=pl.Buffered(1))
    return pl.pallas_call(
        _ffn_kernel,
        grid=(t // tm, hidden // tk),
        in_specs=[x_spec, vec_spec,
                  pl.BlockSpec((d, tk), lambda i, k: (0, k)),
                  pl.BlockSpec((d, tk), lambda i, k: (0, k)),
                  pl.BlockSpec((tk, d), lambda i, k: (k, 0)),
                  vec_spec],
        out_specs=row_spec,
        out_shape=jax.ShapeDtypeStruct((t, d), F32),
        scratch_shapes=[pltpu.VMEM((tm, d), _MXU_DTYPE)],
        compiler_params=_params(("parallel", "arbitrary")),
        name="ffn",
    )(x, gpre, w_gate, w_up, w_down, gpost)


def _split_w_in(w_in):
    sizes = (256, 256, 512, 512, 512, SSD_CONV_DIM, SSD_HEADS, 512, 512, 512, 512, 512, 512)
    offs = np.cumsum((0,) + sizes)
    (r_q, r_k, r_v, r_g, s_z, s_xbc, s_dt, d_q, d_k, d_v, l_q, l_k, l_v) = [
        w_in[:, offs[i]:offs[i + 1]] for i in range(len(sizes))]
    w_qkv = jnp.concatenate([r_q, r_k * RET_DK ** -0.5, r_v, d_q * DIFF_HD ** -0.5, d_k, d_v], axis=1)
    w_dil = jnp.concatenate([l_q * DIL_HD ** -0.5, l_k, l_v], axis=1)
    pad = jnp.zeros((w_in.shape[0], GATE_COLS - _DT_SSD - SSD_HEADS), w_in.dtype)
    w_gates = jnp.concatenate([r_g, s_z, s_xbc, s_dt, pad], axis=1)
    return w_qkv.astype(_MXU_DTYPE), w_dil.astype(_MXU_DTYPE), w_gates.astype(_MXU_DTYPE)


def _mixer(x2, layer, gain_pre, w_in, ret_norm, conv_w, conv_b, dt_bias, a_log, d_skip, ssd_norm,
           diff_lambda, diff_norm, batch, seq):
    w_qkv, w_dil, w_gates = _split_w_in(w_in)
    qkv = _norm_matmul(x2, gain_pre, w_qkv, _MXU_DTYPE)
    dils = _dil_proj(x2, gain_pre, w_dil)
    gates = _norm_matmul(x2, gain_pre, w_gates, F32)
    out_a = _retention(qkv, gates, ret_norm, batch, seq)
    out_b = _ssd(gates, conv_w, conv_b, dt_bias, a_log, d_skip, ssd_norm, batch, seq)
    lam_init = 0.8 - 0.6 * math.exp(-0.3 * layer)
    out_c = _diff_attention(qkv, diff_lambda, diff_norm, lam_init, batch, seq)
    out_d = _dilated_attention(dils, batch, seq)
    return out_a, out_b, out_c, out_d


def kernel(x, norm_mix_pre, w_in, ret_norm, ssd_conv_w, ssd_conv_b, ssd_dt_bias, ssd_a_log, ssd_d, ssd_norm, diff_lambda, diff_norm, w_out, norm_mix_post, norm_ffn_pre, w_gate, w_up, w_down, norm_ffn_post):
    batch, seq, d = x.shape
    assert d == D_MODEL and seq % DIL_SPAN == 0
    depth = w_in.shape[0]
    x2 = x.reshape(batch * seq, d)
    vec = lambda g: g.reshape(1, d).astype(F32)
    for l in range(depth):
        parts = _mixer(x2, l, vec(norm_mix_pre[l]), w_in[l], ret_norm[l], ssd_conv_w[l], ssd_conv_b[l],
                       ssd_dt_bias[l], ssd_a_log[l], ssd_d[l], ssd_norm[l], diff_lambda[l], diff_norm[l],
                       batch, seq)
        x2 = _outproj(parts, w_out[l].astype(_MXU_DTYPE), x2, vec(norm_mix_post[l]))
        x2 = _ffn(x2, vec(norm_ffn_pre[l]), w_gate[l].astype(_MXU_DTYPE), w_up[l].astype(_MXU_DTYPE),
                  w_down[l].astype(_MXU_DTYPE), vec(norm_ffn_post[l]))
    return x2.reshape(batch, seq, d)
```

```python
import functools
import math

import numpy as np
import jax
import jax.numpy as jnp
from jax import lax
from jax.experimental import pallas as pl
from jax.experimental.pallas import tpu as pltpu

F32 = jnp.float32
_MXU_DTYPE = jnp.bfloat16

D_MODEL = 2048
GROUP_WIDTH = D_MODEL // 4
NORM_EPS = 1e-6

RET_HEADS = 4
RET_DK = 64
RET_DV = 128
CHUNK = 128

SSD_HEADS = 8
SSD_HEADDIM = 64
SSD_GROUPS = 2
SSD_STATE = 128
SSD_CONV = 4
SSD_CONV_DIM = 1024

DIFF_HEADS = 4
DIFF_HD = 64
DIFF_BLOCK = 512
DIFF_ROWS = 64

DIL_HEADS = 8
DIL_HD = 64
DIL_DILATIONS = (1, 4, 16)
DIL_SPAN = CHUNK * max(DIL_DILATIONS)
DIL_TILE = 512

FFN_HIDDEN = 5632

QKV_COLS = 2560
_Q_RET, _K_RET, _V_RET = 0, 256, 512
_Q_DIFF, _K_DIFF, _V_DIFF = 1024, 1536, 2048
DIL_COLS = 3 * GROUP_WIDTH
GATE_COLS = 2304
_G_RET, _Z_SSD, _XBC_SSD, _DT_SSD = 0, 512, 1024, 2048
LANES = 128

VMEM_LIMIT_BYTES = 56 * 1024 * 1024


def _params(semantics):
    return pltpu.CompilerParams(dimension_semantics=semantics, vmem_limit_bytes=VMEM_LIMIT_BYTES)


def _rms(x, gain):
    ms = jnp.mean(x * x, axis=-1, keepdims=True)
    return x * lax.rsqrt(ms + NORM_EPS) * gain


def _dot(a, b):
    return jnp.dot(a.astype(_MXU_DTYPE), b.astype(_MXU_DTYPE), preferred_element_type=F32)


def _dot_nt(a, b):
    return lax.dot_general(a.astype(_MXU_DTYPE), b.astype(_MXU_DTYPE), (((1,), (1,)), ((), ())),
                           preferred_element_type=F32)


def _silu(x):
    return x * jax.nn.sigmoid(x)


def _norm_matmul_kernel(x_ref, g_ref, w_ref, o_ref):
    h = _rms(x_ref[...], g_ref[...]).astype(_MXU_DTYPE)
    o_ref[...] = jnp.dot(h, w_ref[...], preferred_element_type=F32).astype(o_ref.dtype)


def _norm_matmul(x, gain, w, out_dtype):
    t, d = x.shape
    n = w.shape[1]
    tm = 512
    return pl.pallas_call(
        _norm_matmul_kernel,
        grid=(t // tm,),
        in_specs=[pl.BlockSpec((tm, d), lambda i: (i, 0)),
                  pl.BlockSpec((1, d), lambda i: (0, 0)),
                  pl.BlockSpec((d, n), lambda i: (0, 0))],
        out_specs=pl.BlockSpec((tm, n), lambda i: (i, 0)),
        out_shape=jax.ShapeDtypeStruct((t, n), out_dtype),
        compiler_params=_params(("parallel",)),
        name="norm_matmul",
    )(x, gain, w)


def _dil_proj_kernel(x_ref, g_ref, w_ref, nat_ref, *rest):
    strided_refs, res_ref = rest[:-1], rest[-1]
    h = _rms(x_ref[...], g_ref[...]).astype(_MXU_DTYPE)
    res = jnp.dot(h, w_ref[...], preferred_element_type=F32)
    nat_ref[...] = res.astype(nat_ref.dtype)
    nblk = res.shape[1] // LANES
    for c in range(nblk):
        res_ref[c] = res[:, c * LANES:(c + 1) * LANES]
    for d, out_ref in zip(DIL_DILATIONS[1:], strided_refs):
        n = DIL_TILE // d
        for r in range(d):
            for c in range(nblk):
                out_ref[r * n:(r + 1) * n, c * LANES:(c + 1) * LANES] = (
                    res_ref[c, pl.ds(r, n, stride=d), :].astype(out_ref.dtype))


def _dil_proj(x, gain, w):
    t, d = x.shape
    n = w.shape[1]
    tm = DIL_TILE
    n_out = len(DIL_DILATIONS)
    out_spec = pl.BlockSpec((tm, n), lambda i: (i, 0))
    return pl.pallas_call(
        _dil_proj_kernel,
        grid=(t // tm,),
        in_specs=[pl.BlockSpec((tm, d), lambda i: (i, 0)),
                  pl.BlockSpec((1, d), lambda i: (0, 0)),
                  pl.BlockSpec((d, n), lambda i: (0, 0))],
        out_specs=[out_spec] * n_out,
        out_shape=[jax.ShapeDtypeStruct((t, n), _MXU_DTYPE)] * n_out,
        scratch_shapes=[pltpu.VMEM((n // LANES, tm, LANES), F32)],
        compiler_params=_params(("parallel",)),
        name="dil_proj",
    )(x, gain, w)


def _retention_tables():
    h = np.arange(RET_HEADS, dtype=np.float32)
    log_gamma = np.log1p(-np.exp2(-5.0 - h)).astype(np.float32)
    pos = np.arange(CHUNK, dtype=np.float32)
    rel = pos[:, None] - pos[None, :]
    intra = np.where(rel >= 0, np.exp(log_gamma[:, None, None] * np.maximum(rel, 0.0)), 0.0)
    zeta = np.exp(log_gamma[:, None] * (CHUNK - 1.0 - pos)[None, :])
    xi = np.exp(log_gamma[:, None] * (pos + 1.0)[None, :])
    decay = np.exp(log_gamma * CHUNK)
    zeta_b = np.broadcast_to(zeta[:, :, None], (RET_HEADS, CHUNK, RET_DK))
    xi_b = np.broadcast_to(xi[:, :, None], (RET_HEADS, CHUNK, RET_DV))
    return (jnp.asarray(intra, F32), jnp.asarray(zeta_b, F32), jnp.asarray(xi_b, F32),
            [float(v) for v in decay.astype(np.float32)])


def _retention_kernel(q_ref, k_ref, v_ref, g_ref, intra_ref, zeta_ref, xi_ref, gain_ref, o_ref,
                      state_ref, *, chunks, decay):
    @pl.when(pl.program_id(1) == 0)
    def _():
        state_ref[...] = jnp.zeros_like(state_ref)

    for c in range(chunks):
        rows = pl.ds(c * CHUNK, CHUNK)
        for h in range(RET_HEADS):
            q = q_ref[rows, h * RET_DK:(h + 1) * RET_DK]
            k = k_ref[rows, h * RET_DK:(h + 1) * RET_DK]
            v = v_ref[rows, h * RET_DV:(h + 1) * RET_DV]
            state = state_ref[h]
            scores = _dot_nt(q, k) * intra_ref[h]
            y = _dot(scores, v) + _dot(q, state) * xi_ref[h]
            kz = k.astype(F32) * zeta_ref[h]
            state_ref[h] = state * decay[h] + _dot(kz.T, v)
            y = _rms(y, gain_ref[:, h * RET_DV:(h + 1) * RET_DV])
            o_ref[rows, h * RET_DV:(h + 1) * RET_DV] = (
                y * _silu(g_ref[rows, h * RET_DV:(h + 1) * RET_DV])).astype(o_ref.dtype)


def _retention(qkv, gates, ret_norm, batch, seq):
    rows = 512
    chunks = rows // CHUNK
    nblk = seq // rows
    intra, zeta_b, xi_b, decay = _retention_tables()
    row_map = lambda col: (lambda b, i: (b * nblk + i, col))
    const3 = lambda b, i: (0, 0, 0)
    return pl.pallas_call(
        functools.partial(_retention_kernel, chunks=chunks, decay=decay),
        grid=(batch, nblk),
        in_specs=[pl.BlockSpec((rows, 256), row_map(_Q_RET // 256)),
                  pl.BlockSpec((rows, 256), row_map(_K_RET // 256)),
                  pl.BlockSpec((rows, 512), row_map(_V_RET // 512)),
                  pl.BlockSpec((rows, 512), row_map(_G_RET // 512)),
                  pl.BlockSpec((RET_HEADS, CHUNK, CHUNK), const3),
                  pl.BlockSpec((RET_HEADS, CHUNK, RET_DK), const3),
                  pl.BlockSpec((RET_HEADS, CHUNK, RET_DV), const3),
                  pl.BlockSpec((1, GROUP_WIDTH), lambda b, i: (0, 0))],
        out_specs=pl.BlockSpec((rows, GROUP_WIDTH), row_map(0)),
        out_shape=jax.ShapeDtypeStruct((batch * seq, GROUP_WIDTH), _MXU_DTYPE),
        scratch_shapes=[pltpu.VMEM((RET_HEADS, RET_DK, RET_DV), F32)],
        compiler_params=_params(("parallel", "arbitrary")),
        name="retention",
    )(qkv, qkv, qkv, gates, intra, zeta_b, xi_b, ret_norm.reshape(1, GROUP_WIDTH))


_SSD_TAIL = 8


def _ssd_kernel(z_ref, xbc_ref, dt_ref, convw_ref, convb_ref, dtb_ref, alog_ref, dskip_ref, gain_ref,
                tril_ref, expand_ref, o_ref, ext_ref, hstate_ref, y_ref):
    @pl.when(pl.program_id(1) == 0)
    def _():
        ext_ref[0:_SSD_TAIL, :] = jnp.zeros((_SSD_TAIL, SSD_CONV_DIM), F32)
        hstate_ref[...] = jnp.zeros_like(hstate_ref)

    ext_ref[_SSD_TAIL:_SSD_TAIL + CHUNK, :] = xbc_ref[...]
    conv = convb_ref[...] + jnp.zeros((CHUNK, SSD_CONV_DIM), F32)
    for k in range(SSD_CONV):
        start = _SSD_TAIL - (SSD_CONV - 1) + k
        conv = conv + ext_ref[start:start + CHUNK, :] * convw_ref[k:k + 1, :]
    ext_ref[0:_SSD_TAIL, :] = ext_ref[CHUNK:CHUNK + _SSD_TAIL, :]
    xbc = _silu(conv)

    inner = SSD_HEADS * SSD_HEADDIM
    gs = SSD_GROUPS * SSD_STATE
    dt = jax.nn.softplus(dt_ref[...] + dtb_ref[...])
    a = dt * (-jnp.exp(alog_ref[...]))
    a_cum = jnp.dot(tril_ref[...], a, preferred_element_type=F32, precision=lax.Precision.HIGHEST)
    a_cum_t = a_cum.T
    last = a_cum[CHUNK - 1:CHUNK, :]
    row_i = lax.broadcasted_iota(jnp.int32, (CHUNK, CHUNK), 0)
    col_i = lax.broadcasted_iota(jnp.int32, (CHUNK, CHUNK), 1)
    causal = row_i >= col_i
    first_of_pair = col_i < SSD_HEADDIM
    pairs_per_group = SSD_HEADS // SSD_GROUPS // 2

    def per_channel(v):
        return jnp.dot(v, expand_ref[...], preferred_element_type=F32, precision=lax.Precision.HIGHEST)

    xs = xbc[:, 0:inner]
    xdt = xs * per_channel(dt)
    from_start = per_channel(jnp.exp(a_cum))
    xw = xdt * per_channel(jnp.exp(last - a_cum))
    chunk_decay = jnp.exp(last)

    for g in range(SSD_GROUPS):
        bm = xbc[:, inner + g * SSD_STATE:inner + (g + 1) * SSD_STATE]
        cm = xbc[:, inner + gs + g * SSD_STATE:inner + gs + (g + 1) * SSD_STATE]
        cb = _dot_nt(cm, bm)
        for r in range(pairs_per_group):
            pair = g * pairs_per_group + r
            lanes = slice(pair * LANES, (pair + 1) * LANES)
            y_diag, decay = [], []
            for h in (2 * pair, 2 * pair + 1):
                seg = a_cum[:, h:h + 1] - a_cum_t[h:h + 1, :]
                decay_in = jnp.where(causal, jnp.exp(jnp.where(causal, seg, 0.0)), 0.0)
                y_diag.append(_dot(cb * decay_in, xdt[:, lanes]))
                decay.append(jnp.broadcast_to(chunk_decay[:, h:h + 1], (CHUNK, CHUNK)))
            hprev = hstate_ref[pair]
            y = jnp.where(first_of_pair, y_diag[0], y_diag[1]) + _dot_nt(cm, hprev) * from_start[:, lanes]
            hstate_ref[pair] = hprev * jnp.where(row_i < SSD_HEADDIM, decay[0], decay[1]) + _dot(xw[:, lanes].T, bm)
            y_ref[:, lanes] = y + xs[:, lanes] * dskip_ref[:, lanes]

    y = y_ref[...] * _silu(z_ref[...])
    gw = inner // SSD_GROUPS
    for g in range(SSD_GROUPS):
        o_ref[:, g * gw:(g + 1) * gw] = _rms(y[:, g * gw:(g + 1) * gw],
                                             gain_ref[:, g * gw:(g + 1) * gw]).astype(o_ref.dtype)


def _pad_lanes(v):
    return jnp.pad(v.astype(F32), (0, LANES - v.shape[0])).reshape(1, LANES)


def _ssd(gates, conv_w, conv_b, dt_bias, a_log, d_skip, ssd_norm, batch, seq):
    nblk = seq // CHUNK
    row_map = lambda col: (lambda b, i: (b * nblk + i, col))
    const2 = lambda b, i: (0, 0)
    tril = jnp.asarray(np.tril(np.ones((CHUNK, CHUNK), np.float32)))
    expand = np.zeros((LANES, GROUP_WIDTH), np.float32)
    for h in range(SSD_HEADS):
        expand[h, h * SSD_HEADDIM:(h + 1) * SSD_HEADDIM] = 1.0
    return pl.pallas_call(
        _ssd_kernel,
        grid=(batch, nblk),
        in_specs=[pl.BlockSpec((CHUNK, 512), row_map(_Z_SSD // 512)),
                  pl.BlockSpec((CHUNK, SSD_CONV_DIM), row_map(_XBC_SSD // SSD_CONV_DIM)),
                  pl.BlockSpec((CHUNK, LANES), row_map(_DT_SSD // LANES)),
                  pl.BlockSpec((SSD_CONV, SSD_CONV_DIM), const2),
                  pl.BlockSpec((1, SSD_CONV_DIM), const2),
                  pl.BlockSpec((1, LANES), const2),
                  pl.BlockSpec((1, LANES), const2),
                  pl.BlockSpec((1, GROUP_WIDTH), const2),
                  pl.BlockSpec((1, GROUP_WIDTH), const2),
                  pl.BlockSpec((CHUNK, CHUNK), const2),
                  pl.BlockSpec((LANES, GROUP_WIDTH), const2)],
        out_specs=pl.BlockSpec((CHUNK, GROUP_WIDTH), row_map(0)),
        out_shape=jax.ShapeDtypeStruct((batch * seq, GROUP_WIDTH), _MXU_DTYPE),
        scratch_shapes=[pltpu.VMEM((_SSD_TAIL + CHUNK, SSD_CONV_DIM), F32),
                        pltpu.VMEM((SSD_HEADS // 2, 2 * SSD_HEADDIM, SSD_STATE), F32),
                        pltpu.VMEM((CHUNK, GROUP_WIDTH), F32)],
        compiler_params=_params(("parallel", "arbitrary")),
        name="ssd",
    )(gates, gates, gates, conv_w.T.astype(F32), conv_b.reshape(1, SSD_CONV_DIM).astype(F32),
      _pad_lanes(dt_bias), _pad_lanes(a_log),
      jnp.repeat(d_skip.astype(F32), SSD_HEADDIM).reshape(1, GROUP_WIDTH),
      ssd_norm.reshape(1, GROUP_WIDTH), tril, jnp.asarray(expand))


def _alibi_slopes(n):
    return np.exp2(-8.0 * (np.arange(n, dtype=np.float32) + 1.0) / n).astype(np.float32)


_ALIBI_FIELDS = (~0x1FF, 0x1E0, 0x1F)
_ALIBI_LANE0 = (DIFF_HD, 0)


def _alibi_tables(seq):
    pos = np.arange(seq)
    n = len(_ALIBI_FIELDS)
    tq = np.zeros((DIFF_HEADS, 2, seq, LANES), np.float32)
    tk = np.zeros((DIFF_HEADS, 2, seq, LANES), np.float32)
    for h, slope in enumerate(_alibi_slopes(DIFF_HEADS)):
        for c, lane0 in enumerate(_ALIBI_LANE0):
            for i, field in enumerate(_ALIBI_FIELDS):
                part = (pos & field).astype(np.float32) * slope
                tq[h, c, :, lane0 + i] = -part
                tq[h, c, :, lane0 + n + i] = 1.0
                tk[h, c, :, lane0 + i] = 1.0
                tk[h, c, :, lane0 + n + i] = part
    return jnp.asarray(tq, _MXU_DTYPE), jnp.asarray(tk, _MXU_DTYPE)


def _diff_kernel(q_ref, k_ref, v_ref, featq_ref, featk_ref, lam_ref, gain_ref, o_ref,
                 qm_ref, km_ref, vone_ref, m_ref, acc_ref, *, lam_init, seq):
    qi = pl.program_id(2)
    tq = tk = DIFF_BLOCK
    lane = lax.broadcasted_iota(jnp.int32, (tq, LANES), 1)

    @pl.when(qi == 0)
    def _():
        for j in range(seq // tk):
            rows = slice(j * tk, (j + 1) * tk)
            k = k_ref[rows, :]
            for c, lane0 in enumerate(_ALIBI_LANE0):
                is_feat = (lane >= lane0) & (lane < lane0 + 2 * len(_ALIBI_FIELDS))
                km_ref[c, rows, :] = jnp.where(is_feat, featk_ref[c, rows, :], k)
        vone_ref[:, 0:LANES] = v_ref[...]
        vone_ref[:, LANES:2 * LANES] = jnp.ones((seq, LANES), vone_ref.dtype)

    q = q_ref[...]
    for c in range(2):
        own = (lane >= c * DIFF_HD) & (lane < (c + 1) * DIFF_HD)
        qm_ref[c] = jnp.where(own, q, featq_ref[c])
    m_ref[...] = jnp.full(m_ref.shape, -jnp.inf, F32)
    acc_ref[...] = jnp.zeros_like(acc_ref)

    def block(row0, nrows, start, width, mask_offset=None):
        v = vone_ref[pl.ds(start, width), :]
        qrows = slice(row0, row0 + nrows)
        for c in range(2):
            s = _dot_nt(qm_ref[c, qrows, :], km_ref[c, pl.ds(start, width), :])
            probs, alphas = [], []
            for r in range(nrows // DIFF_ROWS):
                rows = slice(row0 + r * DIFF_ROWS, row0 + (r + 1) * DIFF_ROWS)
                sr = s[r * DIFF_ROWS:(r + 1) * DIFF_ROWS]
                if mask_offset is not None:
                    row_i = lax.broadcasted_iota(jnp.int32, sr.shape, 0) + (r * DIFF_ROWS + mask_offset)
                    col_i = lax.broadcasted_iota(jnp.int32, sr.shape, 1)
                    sr = jnp.where(row_i >= col_i, sr, -jnp.inf)
                m_prev = m_ref[c, rows]
                m_next = jnp.maximum(m_prev, jnp.max(sr, axis=-1, keepdims=True))
                p = jnp.exp(sr - pltpu.repeat(m_next, width // LANES, axis=1))
                m_ref[c, rows] = m_next
                probs.append(p.astype(_MXU_DTYPE))
                alphas.append(jnp.exp(m_prev - m_next))
            alpha = pltpu.repeat(jnp.concatenate(alphas, axis=0), 2, axis=1)
            acc_ref[c, qrows, :] = alpha * acc_ref[c, qrows, :] + _dot(jnp.concatenate(probs, axis=0), v)

    def wide_body(j, carry):
        block(0, tq, pl.multiple_of(j * (2 * tk), 2 * tk), 2 * tk)
        return carry

    lax.fori_loop(0, qi // 2, wide_body, 0)

    half = tq // 2

    @pl.when(qi % 2 == 0)
    def _():
        block(0, tq, pl.multiple_of(qi * tk, tk), tk, mask_offset=0)

    @pl.when(qi % 2 == 1)
    def _():
        start = pl.multiple_of((qi - 1) * tk, tk)
        block(0, half, start, tk + half, mask_offset=tk)
        block(half, half, start, 2 * tk, mask_offset=tk + half)

    lf = lam_ref[...]
    lam = (jnp.exp(jnp.sum(lf[0:1] * lf[1:2], axis=-1, keepdims=True))
           - jnp.exp(jnp.sum(lf[2:3] * lf[3:4], axis=-1, keepdims=True)) + lam_init)
    att = [acc_ref[c, :, 0:LANES] / acc_ref[c, :, LANES:2 * LANES] for c in range(2)]
    o = att[0] - lam * att[1]
    o_ref[...] = (_rms(o, gain_ref[...]) * (1.0 - lam_init)).astype(o_ref.dtype)


def _diff_attention(qkv, diff_lambda, diff_norm, lam_init, batch, seq):
    nq = seq // DIFF_BLOCK
    assert seq // DIFF_BLOCK <= 256
    featq, featk = _alibi_tables(seq)
    seq_spec = lambda col: pl.BlockSpec((seq, LANES), lambda b, h, qi: (b, col + h))
    const2 = lambda b, h, qi: (0, 0)
    return pl.pallas_call(
        functools.partial(_diff_kernel, lam_init=lam_init, seq=seq),
        grid=(batch, DIFF_HEADS, nq),
        in_specs=[pl.BlockSpec((DIFF_BLOCK, LANES), lambda b, h, qi: (b * nq + qi, _Q_DIFF // LANES + h)),
                  seq_spec(_K_DIFF // LANES),
                  seq_spec(_V_DIFF // LANES),
                  pl.BlockSpec((None, 2, DIFF_BLOCK, LANES), lambda b, h, qi: (h, 0, qi, 0)),
                  pl.BlockSpec((None, 2, seq, LANES), lambda b, h, qi: (h, 0, 0, 0)),
                  pl.BlockSpec((4, DIFF_HD), const2),
                  pl.BlockSpec((1, 2 * DIFF_HD), const2)],
        out_specs=pl.BlockSpec((DIFF_BLOCK, LANES), lambda b, h, qi: (b * nq + qi, h)),
        out_shape=jax.ShapeDtypeStruct((batch * seq, GROUP_WIDTH), _MXU_DTYPE),
        scratch_shapes=[pltpu.VMEM((2, DIFF_BLOCK, LANES), _MXU_DTYPE),
                        pltpu.VMEM((2, seq, LANES), _MXU_DTYPE),
                        pltpu.VMEM((seq, 2 * LANES), _MXU_DTYPE),
                        pltpu.VMEM((2, DIFF_BLOCK, LANES), F32),
                        pltpu.VMEM((2, DIFF_BLOCK, 2 * LANES), F32)],
        compiler_params=_params(("parallel", "parallel", "arbitrary")),
        name="diff_attention",
    )(qkv, qkv, qkv, featq, featk, diff_lambda.astype(F32), diff_norm.reshape(1, 2 * DIFF_HD).astype(F32))


def _dilated_bias(dilation):
    slopes = _alibi_slopes(DIL_HEADS)
    w = CHUNK
    i = np.arange(w)
    j = np.arange(2 * w)
    steps = w + i[:, None] - j[None, :]
    valid = (steps >= 0) & (steps <= w)
    out = np.empty((DIL_HEADS // 2, 2, 2 * w, 2 * w), np.float32)
    for pair in range(DIL_HEADS // 2):
        for first in range(2):
            ok = valid & ~((first == 1) & (j < w)[None, :])
            for hh in range(2):
                b = -slopes[2 * pair + hh] * (steps * dilation).astype(np.float32)
                out[pair, first, hh * w:(hh + 1) * w, :] = np.where(ok, b, -np.inf)
    return jnp.asarray(out)


def _dilated_rows(ref, dilation, residue, block):
    if dilation == 1:
        return ref[block * CHUNK:(block + 1) * CHUNK, :]
    per = DIL_TILE // dilation
    tiles_per_block = CHUNK // per
    pieces = []
    for t in range(block * tiles_per_block, (block + 1) * tiles_per_block):
        pieces.append(ref[t * DIL_TILE + residue * per:t * DIL_TILE + (residue + 1) * per, :])
    return pieces[0] if len(pieces) == 1 else jnp.concatenate(pieces, axis=0)


def _dilated_kernel(q_ref, kp_ref, kc_ref, vp_ref, vc_ref, bias_ref, o_ref, lse_ref, *, dilation):
    d = dilation
    w = CHUNK
    first_step = pl.program_id(1) == 0
    lane = lax.broadcasted_iota(jnp.int32, (w, LANES), 1)
    even = lane < DIL_HD
    blocks_per_residue = DIL_SPAN // (w * d)
    for tile in range(DIL_SPAN // w):
        r, n = tile // blocks_per_residue, tile % blocks_per_residue
        q_t = _dilated_rows(q_ref, d, r, n)
        kc_t = _dilated_rows(kc_ref, d, r, n)
        vc_t = _dilated_rows(vc_ref, d, r, n)
        if n > 0:
            kp_t = _dilated_rows(kc_ref, d, r, n - 1)
            vp_t = _dilated_rows(vc_ref, d, r, n - 1)
            first = 0
        else:
            kp_t = _dilated_rows(kp_ref, d, r, 0)
            vp_t = _dilated_rows(vp_ref, d, r, 0)
            first = jnp.where(first_step, 1, 0)
        rows = pl.ds(tile * w, w) if d == 1 else pl.ds(n * w * d + r, w, stride=d)
        for pair in range(DIL_HEADS // 2):
            lanes = slice(pair * LANES, (pair + 1) * LANES)
            q = q_t[:, lanes]
            zero = jnp.zeros_like(q)
            qs = jnp.concatenate([jnp.where(even, q, zero), jnp.where(even, zero, q)], axis=0)
            kk = jnp.concatenate([kp_t[:, lanes], kc_t[:, lanes]], axis=0)
            vv = jnp.concatenate([vp_t[:, lanes], vc_t[:, lanes]], axis=0)
            s = _dot_nt(qs, kk) + bias_ref[pair, first]
            m = jnp.max(s, axis=-1, keepdims=True)
            p = jnp.exp(s - m)
            l = jnp.sum(p, axis=-1, keepdims=True)
            o2 = _dot(p, vv) / l
            lse = jnp.broadcast_to(m + jnp.log(l), (2 * w, LANES))
            o_ref[pair, rows, :] = jnp.where(even, o2[:w], o2[w:])
            lse_ref[pair, rows, :] = jnp.where(even, lse[:w], lse[w:])


def _dilated_branch(dil, dilation, batch, seq):
    d = dilation
    nstep = seq // DIL_SPAN
    prev_rows = CHUNK * d if d > 1 else CHUNK
    ratio = DIL_SPAN // prev_rows
    cur = lambda col: pl.BlockSpec((DIL_SPAN, GROUP_WIDTH), lambda b, i: (b * nstep + i, col))
    prev = lambda col: pl.BlockSpec(
        (prev_rows, GROUP_WIDTH), lambda b, i: (jnp.maximum((b * nstep + i) * ratio - 1, 0), col))
    pairs = DIL_HEADS // 2
    out_spec = pl.BlockSpec((pairs, DIL_SPAN, LANES), lambda b, i: (0, b * nstep + i, 0))
    return pl.pallas_call(
        functools.partial(_dilated_kernel, dilation=d),
        grid=(batch, nstep),
        in_specs=[cur(0), prev(1), cur(1), prev(2), cur(2),
                  pl.BlockSpec((DIL_HEADS // 2, 2, 2 * CHUNK, 2 * CHUNK), lambda b, i: (0, 0, 0, 0))],
        out_specs=[out_spec, out_spec],
        out_shape=[jax.ShapeDtypeStruct((pairs, batch * seq, LANES), F32)] * 2,
        compiler_params=_params(("parallel", "arbitrary")),
        name=f"dilated_d{d}",
    )(dil, dil, dil, dil, dil, _dilated_bias(d))


def _dilated_combine_kernel(o1, o2, o3, l1, l2, l3, out_ref):
    a, b, c = l1[...], l2[...], l3[...]
    mx = jnp.maximum(jnp.maximum(a, b), c)
    ea, eb, ec = jnp.exp(a - mx), jnp.exp(b - mx), jnp.exp(c - mx)
    out = (ea * o1[...] + eb * o2[...] + ec * o3[...]) / (ea + eb + ec)
    for pair in range(out.shape[0]):
        out_ref[:, pair * LANES:(pair + 1) * LANES] = out[pair].astype(out_ref.dtype)


def _dilated_attention(dils, batch, seq):
    parts = [_dilated_branch(dil, d, batch, seq) for dil, d in zip(dils, DIL_DILATIONS)]
    tm = 1024
    spec = pl.BlockSpec((DIL_HEADS // 2, tm, LANES), lambda i: (0, i, 0))
    return pl.pallas_call(
        _dilated_combine_kernel,
        grid=(batch * seq // tm,),
        in_specs=[spec] * 6,
        out_specs=pl.BlockSpec((tm, GROUP_WIDTH), lambda i: (i, 0)),
        out_shape=jax.ShapeDtypeStruct((batch * seq, GROUP_WIDTH), _MXU_DTYPE),
        compiler_params=_params(("parallel",)),
        name="dilated_combine",
    )(parts[0][0], parts[1][0], parts[2][0], parts[0][1], parts[1][1], parts[2][1])


def _outproj_kernel(a_ref, b_ref, c_ref, d_ref, w_ref, x_ref, g_ref, o_ref):
    gw = GROUP_WIDTH
    acc = jnp.dot(a_ref[...], w_ref[0:gw, :], preferred_element_type=F32)
    acc += jnp.dot(b_ref[...], w_ref[gw:2 * gw, :], preferred_element_type=F32)
    acc += jnp.dot(c_ref[...], w_ref[2 * gw:3 * gw, :], preferred_element_type=F32)
    acc += jnp.dot(d_ref[...], w_ref[3 * gw:4 * gw, :], preferred_element_type=F32)
    o_ref[...] = x_ref[...] + _rms(acc, g_ref[...])


def _outproj(parts, w_out, x, gain):
    t, d = x.shape
    tm = 512
    part_spec = pl.BlockSpec((tm, GROUP_WIDTH), lambda i: (i, 0))
    row_spec = pl.BlockSpec((tm, d), lambda i: (i, 0))
    return pl.pallas_call(
        _outproj_kernel,
        grid=(t // tm,),
        in_specs=[part_spec] * 4 + [pl.BlockSpec((d, d), lambda i: (0, 0)), row_spec,
                                    pl.BlockSpec((1, d), lambda i: (0, 0))],
        out_specs=row_spec,
        out_shape=jax.ShapeDtypeStruct((t, d), F32),
        compiler_params=_params(("parallel",)),
        name="outproj",
    )(*parts, w_out, x, gain)


def _ffn_kernel(x_ref, gpre_ref, wg_ref, wu_ref, wd_ref, gpost_ref, o_ref, h_ref):
    k = pl.program_id(1)

    @pl.when(k == 0)
    def _():
        h_ref[...] = _rms(x_ref[...], gpre_ref[...]).astype(h_ref.dtype)
        o_ref[...] = jnp.zeros_like(o_ref)

    h = h_ref[...]
    gate = jnp.dot(h, wg_ref[...], preferred_element_type=F32)
    up = jnp.dot(h, wu_ref[...], preferred_element_type=F32)
    o_ref[...] += jnp.dot((_silu(gate) * up).astype(_MXU_DTYPE), wd_ref[...], preferred_element_type=F32)

    @pl.when(k == pl.num_programs(1) - 1)
    def _():
        o_ref[...] = x_ref[...] + _rms(o_ref[...], gpost_ref[...])


def _ffn(x, gpre, w_gate, w_up, w_down, gpost):
    t, d = x.shape
    hidden = w_gate.shape[1]
    tm, tk = 1024, 512
    row_spec = pl.BlockSpec((tm, d), lambda i, k: (i, 0))
    vec_spec = pl.BlockSpec((1, d), lambda i, k: (0, 0))
    x_spec = pl.BlockSpec((tm, d), lambda i, k: (i, 0), pipeline_mode=pl.Buffered(1))
    return pl.pallas_call(
        _ffn_kernel,
        grid=(t // tm, hidden // tk),
        in_specs=[x_spec, vec_spec,
                  pl.BlockSpec((d, tk), lambda i, k: (0, k)),
                  pl.BlockSpec((d, tk), lambda i, k: (0, k)),
                  pl.BlockSpec((tk, d), lambda i, k: (k, 0)),
                  vec_spec],
        out_specs=row_spec,
        out_shape=jax.ShapeDtypeStruct((t, d), F32),
        scratch_shapes=[pltpu.VMEM((tm, d), _MXU_DTYPE)],
        compiler_params=_params(("parallel", "arbitrary")),
        name="ffn",
    )(x, gpre, w_gate, w_up, w_down, gpost)


def _split_w_in(w_in):
    sizes = (256, 256, 512, 512, 512, SSD_CONV_DIM, SSD_HEADS, 512, 512, 512, 512, 512, 512)
    offs = np.cumsum((0,) + sizes)
    (r_q, r_k, r_v, r_g, s_z, s_xbc, s_dt, d_q, d_k, d_v, l_q, l_k, l_v) = [
        w_in[:, offs[i]:offs[i + 1]] for i in range(len(sizes))]
    w_qkv = jnp.concatenate([r_q, r_k * RET_DK ** -0.5, r_v, d_q * DIFF_HD ** -0.5, d_k, d_v], axis=1)
    w_dil = jnp.concatenate([l_q * DIL_HD ** -0.5, l_k, l_v], axis=1)
    pad = jnp.zeros((w_in.shape[0], GATE_COLS - _DT_SSD - SSD_HEADS), w_in.dtype)
    w_gates = jnp.concatenate([r_g, s_z, s_xbc, s_dt, pad], axis=1)
    return w_qkv.astype(_MXU_DTYPE), w_dil.astype(_MXU_DTYPE), w_gates.astype(_MXU_DTYPE)


def _mixer(x2, layer, gain_pre, w_in, ret_norm, conv_w, conv_b, dt_bias, a_log, d_skip, ssd_norm,
           diff_lambda, diff_norm, batch, seq):
    w_qkv, w_dil, w_gates = _split_w_in(w_in)
    qkv = _norm_matmul(x2, gain_pre, w_qkv, _MXU_DTYPE)
    dils = _dil_proj(x2, gain_pre, w_dil)
    gates = _norm_matmul(x2, gain_pre, w_gates, F32)
    out_a = _retention(qkv, gates, ret_norm, batch, seq)
    out_b = _ssd(gates, conv_w, conv_b, dt_bias, a_log, d_skip, ssd_norm, batch, seq)
    lam_init = 0.8 - 0.6 * math.exp(-0.3 * layer)
    out_c = _diff_attention(qkv, diff_lambda, diff_norm, lam_init, batch, seq)
    out_d = _dilated_attention(dils, batch, seq)
    return out_a, out_b, out_c, out_d


def kernel(x, norm_mix_pre, w_in, ret_norm, ssd_conv_w, ssd_conv_b, ssd_dt_bias, ssd_a_log, ssd_d, ssd_norm, diff_lambda, diff_norm, w_out, norm_mix_post, norm_ffn_pre, w_gate, w_up, w_down, norm_ffn_post):
    batch, seq, d = x.shape
    assert d == D_MODEL and seq % DIL_SPAN == 0
    depth = w_in.shape[0]
    x2 = x.reshape(batch * seq, d)
    vec = lambda g: g.reshape(1, d).astype(F32)
    for l in range(depth):
        parts = _mixer(x2, l, vec(norm_mix_pre[l]), w_in[l], ret_norm[l], ssd_conv_w[l], ssd_conv_b[l],
                       ssd_dt_bias[l], ssd_a_log[l], ssd_d[l], ssd_norm[l], diff_lambda[l], diff_norm[l],
                       batch, seq)
        x2 = _outproj(parts, w_out[l].astype(_MXU_DTYPE), x2, vec(norm_mix_post[l]))
        x2 = _ffn(x2, vec(norm_ffn_pre[l]), w_gate[l].astype(_MXU_DTYPE), w_up[l].astype(_MXU_DTYPE),
                  w_down[l].astype(_MXU_DTYPE), vec(norm_ffn_post[l]))
    return x2.reshape(batch, seq, d)
```

```python
import functools
import math

import numpy as np
import jax
import jax.numpy as jnp
from jax import lax
from jax.experimental import pallas as pl
from jax.experimental.pallas import tpu as pltpu

F32 = jnp.float32
_MXU_DTYPE = jnp.bfloat16

D_MODEL = 2048
GROUP_WIDTH = D_MODEL // 4
NORM_EPS = 1e-6

RET_HEADS = 4
RET_DK = 64
RET_DV = 128
CHUNK = 128

SSD_HEADS = 8
SSD_HEADDIM = 64
SSD_GROUPS = 2
SSD_STATE = 128
SSD_CONV = 4
SSD_CONV_DIM = 1024

DIFF_HEADS = 4
DIFF_HD = 64
DIFF_BLOCK = 512
DIFF_ROWS = 64

DIL_HEADS = 8
DIL_HD = 64
DIL_DILATIONS = (1, 4, 16)
DIL_SPAN = CHUNK * max(DIL_DILATIONS)
DIL_TILE = 512
DIL_ROWS = 64

FFN_HIDDEN = 5632

QKV_COLS = 2560
_Q_RET, _K_RET, _V_RET = 0, 256, 512
_Q_DIFF, _K_DIFF, _V_DIFF = 1024, 1536, 2048
DIL_COLS = 3 * GROUP_WIDTH
GATE_COLS = 2304
_G_RET, _Z_SSD, _XBC_SSD, _DT_SSD = 0, 512, 1024, 2048
LANES = 128

VMEM_LIMIT_BYTES = 56 * 1024 * 1024


def _params(semantics):
    return pltpu.CompilerParams(dimension_semantics=semantics, vmem_limit_bytes=VMEM_LIMIT_BYTES)


def _rms(x, gain):
    ms = jnp.mean(x * x, axis=-1, keepdims=True)
    return x * lax.rsqrt(ms + NORM_EPS) * gain


def _dot(a, b):
    return jnp.dot(a.astype(_MXU_DTYPE), b.astype(_MXU_DTYPE), preferred_element_type=F32)


def _dot_nt(a, b):
    return lax.dot_general(a.astype(_MXU_DTYPE), b.astype(_MXU_DTYPE), (((1,), (1,)), ((), ())),
                           preferred_element_type=F32)


def _silu(x):
    return x * jax.nn.sigmoid(x)


def _norm_matmul_kernel(x_ref, g_ref, w_ref, o_ref):
    h = _rms(x_ref[...], g_ref[...]).astype(_MXU_DTYPE)
    o_ref[...] = jnp.dot(h, w_ref[...], preferred_element_type=F32).astype(o_ref.dtype)


def _norm_matmul(x, gain, w, out_dtype):
    t, d = x.shape
    n = w.shape[1]
    tm = 512
    return pl.pallas_call(
        _norm_matmul_kernel,
        grid=(t // tm,),
        in_specs=[pl.BlockSpec((tm, d), lambda i: (i, 0)),
                  pl.BlockSpec((1, d), lambda i: (0, 0)),
                  pl.BlockSpec((d, n), lambda i: (0, 0))],
        out_specs=pl.BlockSpec((tm, n), lambda i: (i, 0)),
        out_shape=jax.ShapeDtypeStruct((t, n), out_dtype),
        compiler_params=_params(("parallel",)),
        name="norm_matmul",
    )(x, gain, w)


def _dil_proj_kernel(x_ref, g_ref, w_ref, nat_ref, *rest):
    strided_refs, res_ref = rest[:-1], rest[-1]
    h = _rms(x_ref[...], g_ref[...]).astype(_MXU_DTYPE)
    res = jnp.dot(h, w_ref[...], preferred_element_type=F32)
    nat_ref[...] = res.astype(nat_ref.dtype)
    nblk = res.shape[1] // LANES
    for c in range(nblk):
        res_ref[c] = res[:, c * LANES:(c + 1) * LANES]
    for d, out_ref in zip(DIL_DILATIONS[1:], strided_refs):
        n = DIL_TILE // d
        for r in range(d):
            for c in range(nblk):
                out_ref[r * n:(r + 1) * n, c * LANES:(c + 1) * LANES] = (
                    res_ref[c, pl.ds(r, n, stride=d), :].astype(out_ref.dtype))


def _dil_proj(x, gain, w):
    t, d = x.shape
    n = w.shape[1]
    tm = DIL_TILE
    n_out = len(DIL_DILATIONS)
    out_spec = pl.BlockSpec((tm, n), lambda i: (i, 0))
    return pl.pallas_call(
        _dil_proj_kernel,
        grid=(t // tm,),
        in_specs=[pl.BlockSpec((tm, d), lambda i: (i, 0)),
                  pl.BlockSpec((1, d), lambda i: (0, 0)),
                  pl.BlockSpec((d, n), lambda i: (0, 0))],
        out_specs=[out_spec] * n_out,
        out_shape=[jax.ShapeDtypeStruct((t, n), _MXU_DTYPE)] * n_out,
        scratch_shapes=[pltpu.VMEM((n // LANES, tm, LANES), F32)],
        compiler_params=_params(("parallel",)),
        name="dil_proj",
    )(x, gain, w)


def _retention_tables():
    h = np.arange(RET_HEADS, dtype=np.float32)
    log_gamma = np.log1p(-np.exp2(-5.0 - h)).astype(np.float32)
    pos = np.arange(CHUNK, dtype=np.float32)
    rel = pos[:, None] - pos[None, :]
    intra = np.where(rel >= 0, np.exp(log_gamma[:, None, None] * np.maximum(rel, 0.0)), 0.0)
    zeta = np.exp(log_gamma[:, None] * (CHUNK - 1.0 - pos)[None, :])
    xi = np.exp(log_gamma[:, None] * (pos + 1.0)[None, :])
    decay = np.exp(log_gamma * CHUNK)
    zeta_b = np.broadcast_to(zeta[:, :, None], (RET_HEADS, CHUNK, RET_DK))
    xi_b = np.broadcast_to(xi[:, :, None], (RET_HEADS, CHUNK, RET_DV))
    return (jnp.asarray(intra, F32), jnp.asarray(zeta_b, F32), jnp.asarray(xi_b, F32),
            [float(v) for v in decay.astype(np.float32)])


def _retention_kernel(q_ref, k_ref, v_ref, g_ref, intra_ref, zeta_ref, xi_ref, gain_ref, o_ref,
                      state_ref, *, chunks, decay):
    @pl.when(pl.program_id(1) == 0)
    def _():
        state_ref[...] = jnp.zeros_like(state_ref)

    for c in range(chunks):
        rows = pl.ds(c * CHUNK, CHUNK)
        for h in range(RET_HEADS):
            q = q_ref[rows, h * RET_DK:(h + 1) * RET_DK]
            k = k_ref[rows, h * RET_DK:(h + 1) * RET_DK]
            v = v_ref[rows, h * RET_DV:(h + 1) * RET_DV]
            state = state_ref[h]
            scores = _dot_nt(q, k) * intra_ref[h]
            y = _dot(scores, v) + _dot(q, state) * xi_ref[h]
            kz = k.astype(F32) * zeta_ref[h]
            state_ref[h] = state * decay[h] + _dot(kz.T, v)
            y = _rms(y, gain_ref[:, h * RET_DV:(h + 1) * RET_DV])
            o_ref[rows, h * RET_DV:(h + 1) * RET_DV] = (
                y * _silu(g_ref[rows, h * RET_DV:(h + 1) * RET_DV])).astype(o_ref.dtype)


def _retention(qkv, gates, ret_norm, batch, seq):
    rows = 512
    chunks = rows // CHUNK
    nblk = seq // rows
    intra, zeta_b, xi_b, decay = _retention_tables()
    row_map = lambda col: (lambda b, i: (b * nblk + i, col))
    const3 = lambda b, i: (0, 0, 0)
    return pl.pallas_call(
        functools.partial(_retention_kernel, chunks=chunks, decay=decay),
        grid=(batch, nblk),
        in_specs=[pl.BlockSpec((rows, 256), row_map(_Q_RET // 256)),
                  pl.BlockSpec((rows, 256), row_map(_K_RET // 256)),
                  pl.BlockSpec((rows, 512), row_map(_V_RET // 512)),
                  pl.BlockSpec((rows, 512), row_map(_G_RET // 512)),
                  pl.BlockSpec((RET_HEADS, CHUNK, CHUNK), const3),
                  pl.BlockSpec((RET_HEADS, CHUNK, RET_DK), const3),
                  pl.BlockSpec((RET_HEADS, CHUNK, RET_DV), const3),
                  pl.BlockSpec((1, GROUP_WIDTH), lambda b, i: (0, 0))],
        out_specs=pl.BlockSpec((rows, GROUP_WIDTH), row_map(0)),
        out_shape=jax.ShapeDtypeStruct((batch * seq, GROUP_WIDTH), _MXU_DTYPE),
        scratch_shapes=[pltpu.VMEM((RET_HEADS, RET_DK, RET_DV), F32)],
        compiler_params=_params(("parallel", "arbitrary")),
        name="retention",
    )(qkv, qkv, qkv, gates, intra, zeta_b, xi_b, ret_norm.reshape(1, GROUP_WIDTH))


_SSD_TAIL = 8


def _ssd_kernel(z_ref, xbc_ref, dt_ref, convw_ref, convb_ref, dtb_ref, alog_ref, dskip_ref, gain_ref,
                tril_ref, expand_ref, o_ref, ext_ref, hstate_ref, y_ref):
    @pl.when(pl.program_id(1) == 0)
    def _():
        ext_ref[0:_SSD_TAIL, :] = jnp.zeros((_SSD_TAIL, SSD_CONV_DIM), F32)
        hstate_ref[...] = jnp.zeros_like(hstate_ref)

    ext_ref[_SSD_TAIL:_SSD_TAIL + CHUNK, :] = xbc_ref[...]
    conv = convb_ref[...] + jnp.zeros((CHUNK, SSD_CONV_DIM), F32)
    for k in range(SSD_CONV):
        start = _SSD_TAIL - (SSD_CONV - 1) + k
        conv = conv + ext_ref[start:start + CHUNK, :] * convw_ref[k:k + 1, :]
    ext_ref[0:_SSD_TAIL, :] = ext_ref[CHUNK:CHUNK + _SSD_TAIL, :]
    xbc = _silu(conv)

    inner = SSD_HEADS * SSD_HEADDIM
    gs = SSD_GROUPS * SSD_STATE
    dt = jax.nn.softplus(dt_ref[...] + dtb_ref[...])
    a = dt * (-jnp.exp(alog_ref[...]))
    a_cum = jnp.dot(tril_ref[...], a, preferred_element_type=F32, precision=lax.Precision.HIGHEST)
    a_cum_t = a_cum.T
    last = a_cum[CHUNK - 1:CHUNK, :]
    row_i = lax.broadcasted_iota(jnp.int32, (CHUNK, CHUNK), 0)
    col_i = lax.broadcasted_iota(jnp.int32, (CHUNK, CHUNK), 1)
    causal = row_i >= col_i
    first_of_pair = col_i < SSD_HEADDIM
    pairs_per_group = SSD_HEADS // SSD_GROUPS // 2

    def per_channel(v):
        return jnp.dot(v, expand_ref[...], preferred_element_type=F32, precision=lax.Precision.HIGHEST)

    xs = xbc[:, 0:inner]
    xdt = xs * per_channel(dt)
    from_start = per_channel(jnp.exp(a_cum))
    xw = xdt * per_channel(jnp.exp(last - a_cum))
    chunk_decay = jnp.exp(last)

    for g in range(SSD_GROUPS):
        bm = xbc[:, inner + g * SSD_STATE:inner + (g + 1) * SSD_STATE]
        cm = xbc[:, inner + gs + g * SSD_STATE:inner + gs + (g + 1) * SSD_STATE]
        cb = _dot_nt(cm, bm)
        for r in range(pairs_per_group):
            pair = g * pairs_per_group + r
            lanes = slice(pair * LANES, (pair + 1) * LANES)
            y_diag, decay = [], []
            for h in (2 * pair, 2 * pair + 1):
                seg = a_cum[:, h:h + 1] - a_cum_t[h:h + 1, :]
                decay_in = jnp.where(causal, jnp.exp(jnp.where(causal, seg, 0.0)), 0.0)
                y_diag.append(_dot(cb * decay_in, xdt[:, lanes]))
                decay.append(jnp.broadcast_to(chunk_decay[:, h:h + 1], (CHUNK, CHUNK)))
            hprev = hstate_ref[pair]
            y = jnp.where(first_of_pair, y_diag[0], y_diag[1]) + _dot_nt(cm, hprev) * from_start[:, lanes]
            hstate_ref[pair] = hprev * jnp.where(row_i < SSD_HEADDIM, decay[0], decay[1]) + _dot(xw[:, lanes].T, bm)
            y_ref[:, lanes] = y + xs[:, lanes] * dskip_ref[:, lanes]

    y = y_ref[...] * _silu(z_ref[...])
    gw = inner // SSD_GROUPS
    for g in range(SSD_GROUPS):
        o_ref[:, g * gw:(g + 1) * gw] = _rms(y[:, g * gw:(g + 1) * gw],
                                             gain_ref[:, g * gw:(g + 1) * gw]).astype(o_ref.dtype)


def _pad_lanes(v):
    return jnp.pad(v.astype(F32), (0, LANES - v.shape[0])).reshape(1, LANES)


def _ssd(gates, conv_w, conv_b, dt_bias, a_log, d_skip, ssd_norm, batch, seq):
    nblk = seq // CHUNK
    row_map = lambda col: (lambda b, i: (b * nblk + i, col))
    const2 = lambda b, i: (0, 0)
    tril = jnp.asarray(np.tril(np.ones((CHUNK, CHUNK), np.float32)))
    expand = np.zeros((LANES, GROUP_WIDTH), np.float32)
    for h in range(SSD_HEADS):
        expand[h, h * SSD_HEADDIM:(h + 1) * SSD_HEADDIM] = 1.0
    return pl.pallas_call(
        _ssd_kernel,
        grid=(batch, nblk),
        in_specs=[pl.BlockSpec((CHUNK, 512), row_map(_Z_SSD // 512)),
                  pl.BlockSpec((CHUNK, SSD_CONV_DIM), row_map(_XBC_SSD // SSD_CONV_DIM)),
                  pl.BlockSpec((CHUNK, LANES), row_map(_DT_SSD // LANES)),
                  pl.BlockSpec((SSD_CONV, SSD_CONV_DIM), const2),
                  pl.BlockSpec((1, SSD_CONV_DIM), const2),
                  pl.BlockSpec((1, LANES), const2),
                  pl.BlockSpec((1, LANES), const2),
                  pl.BlockSpec((1, GROUP_WIDTH), const2),
                  pl.BlockSpec((1, GROUP_WIDTH), const2),
                  pl.BlockSpec((CHUNK, CHUNK), const2),
                  pl.BlockSpec((LANES, GROUP_WIDTH), const2)],
        out_specs=pl.BlockSpec((CHUNK, GROUP_WIDTH), row_map(0)),
        out_shape=jax.ShapeDtypeStruct((batch * seq, GROUP_WIDTH), _MXU_DTYPE),
        scratch_shapes=[pltpu.VMEM((_SSD_TAIL + CHUNK, SSD_CONV_DIM), F32),
                        pltpu.VMEM((SSD_HEADS // 2, 2 * SSD_HEADDIM, SSD_STATE), F32),
                        pltpu.VMEM((CHUNK, GROUP_WIDTH), F32)],
        compiler_params=_params(("parallel", "arbitrary")),
        name="ssd",
    )(gates, gates, gates, conv_w.T.astype(F32), conv_b.reshape(1, SSD_CONV_DIM).astype(F32),
      _pad_lanes(dt_bias), _pad_lanes(a_log),
      jnp.repeat(d_skip.astype(F32), SSD_HEADDIM).reshape(1, GROUP_WIDTH),
      ssd_norm.reshape(1, GROUP_WIDTH), tril, jnp.asarray(expand))


def _alibi_slopes(n):
    return np.exp2(-8.0 * (np.arange(n, dtype=np.float32) + 1.0) / n).astype(np.float32)


_ALIBI_FIELDS = (~0x1FF, 0x1E0, 0x1F)
_ALIBI_LANE0 = (DIFF_HD, 0)


def _alibi_tables(seq):
    pos = np.arange(seq)
    n = len(_ALIBI_FIELDS)
    tq = np.zeros((DIFF_HEADS, 2, seq, LANES), np.float32)
    tk = np.zeros((DIFF_HEADS, 2, seq, LANES), np.float32)
    for h, slope in enumerate(_alibi_slopes(DIFF_HEADS)):
        for c, lane0 in enumerate(_ALIBI_LANE0):
            for i, field in enumerate(_ALIBI_FIELDS):
                part = (pos & field).astype(np.float32) * slope
                tq[h, c, :, lane0 + i] = -part
                tq[h, c, :, lane0 + n + i] = 1.0
                tk[h, c, :, lane0 + i] = 1.0
                tk[h, c, :, lane0 + n + i] = part
    return jnp.asarray(tq, _MXU_DTYPE), jnp.asarray(tk, _MXU_DTYPE)


def _diff_kernel(q_ref, k_ref, v_ref, featq_ref, featk_ref, lam_ref, gain_ref, o_ref,
                 qm_ref, km_ref, vone_ref, m_ref, acc_ref, *, lam_init, seq):
    qi = pl.program_id(2)
    tq = tk = DIFF_BLOCK
    lane = lax.broadcasted_iota(jnp.int32, (tq, LANES), 1)

    @pl.when(qi == 0)
    def _():
        for j in range(seq // tk):
            rows = slice(j * tk, (j + 1) * tk)
            k = k_ref[rows, :]
            for c, lane0 in enumerate(_ALIBI_LANE0):
                is_feat = (lane >= lane0) & (lane < lane0 + 2 * len(_ALIBI_FIELDS))
                km_ref[c, rows, :] = jnp.where(is_feat, featk_ref[c, rows, :], k)
        vone_ref[:, 0:LANES] = v_ref[...]
        vone_ref[:, LANES:2 * LANES] = jnp.ones((seq, LANES), vone_ref.dtype)

    q = q_ref[...]
    for c in range(2):
        own = (lane >= c * DIFF_HD) & (lane < (c + 1) * DIFF_HD)
        qm_ref[c] = jnp.where(own, q, featq_ref[c])
    m_ref[...] = jnp.full(m_ref.shape, -jnp.inf, F32)
    acc_ref[...] = jnp.zeros_like(acc_ref)

    def block(row0, nrows, start, width, mask_offset=None):
        v = vone_ref[pl.ds(start, width), :]
        qrows = slice(row0, row0 + nrows)
        kh, rh = width // 2, nrows // 2
        scores = [[_dot_nt(qm_ref[c, qrows, :], km_ref[c, pl.ds(start + i * kh, kh), :]) for i in range(2)]
                  for c in range(2)]
        for c in range(2):
            s_halves = scores[c]
            probs, alphas = [], []
            for r in range(nrows // DIFF_ROWS):
                rows = slice(row0 + r * DIFF_ROWS, row0 + (r + 1) * DIFF_ROWS)
                sr = jnp.concatenate([s[r * DIFF_ROWS:(r + 1) * DIFF_ROWS] for s in s_halves], axis=1)
                if mask_offset is not None:
                    row_i = lax.broadcasted_iota(jnp.int32, sr.shape, 0) + (r * DIFF_ROWS + mask_offset)
                    col_i = lax.broadcasted_iota(jnp.int32, sr.shape, 1)
                    sr = jnp.where(row_i >= col_i, sr, -jnp.inf)
                m_prev = m_ref[c, rows]
                m_next = jnp.maximum(m_prev, jnp.max(sr, axis=-1, keepdims=True))
                p = jnp.exp(sr - pltpu.repeat(m_next, width // LANES, axis=1))
                m_ref[c, rows] = m_next
                probs.append(p.astype(_MXU_DTYPE))
                alphas.append(jnp.exp(m_prev - m_next))
            alpha = pltpu.repeat(jnp.concatenate(alphas, axis=0), 2, axis=1)
            p_all = jnp.concatenate(probs, axis=0)
            pv = jnp.concatenate([_dot(p_all[:rh], v), _dot(p_all[rh:], v)], axis=0)
            acc_ref[c, qrows, :] = alpha * acc_ref[c, qrows, :] + pv

    def wide_body(j, carry):
        block(0, tq, pl.multiple_of(j * (2 * tk), 2 * tk), 2 * tk)
        return carry

    lax.fori_loop(0, qi // 2, wide_body, 0)

    half = tq // 2

    @pl.when(qi % 2 == 0)
    def _():
        block(0, tq, pl.multiple_of(qi * tk, tk), tk, mask_offset=0)

    @pl.when(qi % 2 == 1)
    def _():
        start = pl.multiple_of((qi - 1) * tk, tk)
        block(0, half, start, tk + half, mask_offset=tk)
        block(half, half, start, 2 * tk, mask_offset=tk + half)

    lf = lam_ref[...]
    lam = (jnp.exp(jnp.sum(lf[0:1] * lf[1:2], axis=-1, keepdims=True))
           - jnp.exp(jnp.sum(lf[2:3] * lf[3:4], axis=-1, keepdims=True)) + lam_init)
    att = [acc_ref[c, :, 0:LANES] / acc_ref[c, :, LANES:2 * LANES] for c in range(2)]
    o = att[0] - lam * att[1]
    o_ref[...] = (_rms(o, gain_ref[...]) * (1.0 - lam_init)).astype(o_ref.dtype)


def _diff_attention(qkv, diff_lambda, diff_norm, lam_init, batch, seq):
    nq = seq // DIFF_BLOCK
    assert seq // DIFF_BLOCK <= 256
    featq, featk = _alibi_tables(seq)
    seq_spec = lambda col: pl.BlockSpec((seq, LANES), lambda b, h, qi: (b, col + h))
    const2 = lambda b, h, qi: (0, 0)
    return pl.pallas_call(
        functools.partial(_diff_kernel, lam_init=lam_init, seq=seq),
        grid=(batch, DIFF_HEADS, nq),
        in_specs=[pl.BlockSpec((DIFF_BLOCK, LANES), lambda b, h, qi: (b * nq + qi, _Q_DIFF // LANES + h)),
                  seq_spec(_K_DIFF // LANES),
                  seq_spec(_V_DIFF // LANES),
                  pl.BlockSpec((None, 2, DIFF_BLOCK, LANES), lambda b, h, qi: (h, 0, qi, 0)),
                  pl.BlockSpec((None, 2, seq, LANES), lambda b, h, qi: (h, 0, 0, 0)),
                  pl.BlockSpec((4, DIFF_HD), const2),
                  pl.BlockSpec((1, 2 * DIFF_HD), const2)],
        out_specs=pl.BlockSpec((DIFF_BLOCK, LANES), lambda b, h, qi: (b * nq + qi, h)),
        out_shape=jax.ShapeDtypeStruct((batch * seq, GROUP_WIDTH), _MXU_DTYPE),
        scratch_shapes=[pltpu.VMEM((2, DIFF_BLOCK, LANES), _MXU_DTYPE),
                        pltpu.VMEM((2, seq, LANES), _MXU_DTYPE),
                        pltpu.VMEM((seq, 2 * LANES), _MXU_DTYPE),
                        pltpu.VMEM((2, DIFF_BLOCK, LANES), F32),
                        pltpu.VMEM((2, DIFF_BLOCK, 2 * LANES), F32)],
        compiler_params=_params(("parallel", "parallel", "arbitrary")),
        name="diff_attention",
    )(qkv, qkv, qkv, featq, featk, diff_lambda.astype(F32), diff_norm.reshape(1, 2 * DIFF_HD).astype(F32))


def _dilated_bias(dilation):
    slopes = _alibi_slopes(DIL_HEADS)
    w = CHUNK
    i = np.arange(w)
    j = np.arange(2 * w)
    steps = w + i[:, None] - j[None, :]
    valid = (steps >= 0) & (steps <= w)
    out = np.empty((DIL_HEADS // 2, 2, 2 * w, 2 * w), np.float32)
    for pair in range(DIL_HEADS // 2):
        for first in range(2):
            ok = valid & ~((first == 1) & (j < w)[None, :])
            for hh in range(2):
                b = -slopes[2 * pair + hh] * (steps * dilation).astype(np.float32)
                out[pair, first, hh * w:(hh + 1) * w, :] = np.where(ok, b, -np.inf)
    return jnp.asarray(out)


def _dilated_rows(ref, dilation, residue, block):
    if dilation == 1:
        return ref[block * CHUNK:(block + 1) * CHUNK, :]
    per = DIL_TILE // dilation
    tiles_per_block = CHUNK // per
    pieces = []
    for t in range(block * tiles_per_block, (block + 1) * tiles_per_block):
        pieces.append(ref[t * DIL_TILE + residue * per:t * DIL_TILE + (residue + 1) * per, :])
    return pieces[0] if len(pieces) == 1 else jnp.concatenate(pieces, axis=0)


def _dilated_kernel(q_ref, kp_ref, kc_ref, vp_ref, vc_ref, bias_ref, o_ref, lse_ref, *, dilation):
    d = dilation
    w = CHUNK
    first_step = pl.program_id(1) == 0
    lane = lax.broadcasted_iota(jnp.int32, (w, LANES), 1)
    even = lane < DIL_HD
    blocks_per_residue = DIL_SPAN // (w * d)
    for tile in range(DIL_SPAN // w):
        r, n = tile // blocks_per_residue, tile % blocks_per_residue
        q_t = _dilated_rows(q_ref, d, r, n)
        kc_t = _dilated_rows(kc_ref, d, r, n)
        vc_t = _dilated_rows(vc_ref, d, r, n)
        if n > 0:
            kp_t = _dilated_rows(kc_ref, d, r, n - 1)
            vp_t = _dilated_rows(vc_ref, d, r, n - 1)
            first = 0
        else:
            kp_t = _dilated_rows(kp_ref, d, r, 0)
            vp_t = _dilated_rows(vp_ref, d, r, 0)
            first = jnp.where(first_step, 1, 0)
        rows = pl.ds(tile * w, w) if d == 1 else pl.ds(n * w * d + r, w, stride=d)
        for pair in range(DIL_HEADS // 2):
            lanes = slice(pair * LANES, (pair + 1) * LANES)
            q = q_t[:, lanes]
            zero = jnp.zeros_like(q)
            qs = jnp.concatenate([jnp.where(even, q, zero), jnp.where(even, zero, q)], axis=0)
            kk = jnp.concatenate([kp_t[:, lanes], kc_t[:, lanes]], axis=0)
            vv = jnp.concatenate([vp_t[:, lanes], vc_t[:, lanes]], axis=0)
            s = _dot_nt(qs, kk)
            probs, scales, lses = [], [], []
            for c in range(2 * w // DIL_ROWS):
                crows = slice(c * DIL_ROWS, (c + 1) * DIL_ROWS)
                sc = s[crows] + bias_ref[pair, first, crows, :]
                m = jnp.max(sc, axis=-1, keepdims=True)
                p = jnp.exp(sc - m)
                l = jnp.sum(p, axis=-1, keepdims=True)
                probs.append(p.astype(_MXU_DTYPE))
                scales.append(jnp.broadcast_to(1.0 / l, (DIL_ROWS, LANES)))
                lses.append(jnp.broadcast_to(m + jnp.log(l), (DIL_ROWS, LANES)))
            o2 = _dot(jnp.concatenate(probs, axis=0), vv) * jnp.concatenate(scales, axis=0)
            lse = jnp.concatenate(lses, axis=0)
            o_ref[pair, rows, :] = jnp.where(even, o2[:w], o2[w:])
            lse_ref[pair, rows, :] = jnp.where(even, lse[:w], lse[w:])


def _dilated_branch(dil, dilation, batch, seq):
    d = dilation
    nstep = seq // DIL_SPAN
    prev_rows = CHUNK * d if d > 1 else CHUNK
    ratio = DIL_SPAN // prev_rows
    cur = lambda col: pl.BlockSpec((DIL_SPAN, GROUP_WIDTH), lambda b, i: (b * nstep + i, col))
    prev = lambda col: pl.BlockSpec(
        (prev_rows, GROUP_WIDTH), lambda b, i: (jnp.maximum((b * nstep + i) * ratio - 1, 0), col))
    pairs = DIL_HEADS // 2
    out_spec = pl.BlockSpec((pairs, DIL_SPAN, LANES), lambda b, i: (0, b * nstep + i, 0))
    return pl.pallas_call(
        functools.partial(_dilated_kernel, dilation=d),
        grid=(batch, nstep),
        in_specs=[cur(0), prev(1), cur(1), prev(2), cur(2),
                  pl.BlockSpec((DIL_HEADS // 2, 2, 2 * CHUNK, 2 * CHUNK), lambda b, i: (0, 0, 0, 0))],
        out_specs=[out_spec, out_spec],
        out_shape=[jax.ShapeDtypeStruct((pairs, batch * seq, LANES), F32)] * 2,
        compiler_params=_params(("parallel", "arbitrary")),
        name=f"dilated_d{d}",
    )(dil, dil, dil, dil, dil, _dilated_bias(d))


def _dilated_combine_kernel(o1, o2, o3, l1, l2, l3, out_ref):
    a, b, c = l1[...], l2[...], l3[...]
    mx = jnp.maximum(jnp.maximum(a, b), c)
    ea, eb, ec = jnp.exp(a - mx), jnp.exp(b - mx), jnp.exp(c - mx)
    out = (ea * o1[...] + eb * o2[...] + ec * o3[...]) / (ea + eb + ec)
    for pair in range(out.shape[0]):
        out_ref[:, pair * LANES:(pair + 1) * LANES] = out[pair].astype(out_ref.dtype)


def _dilated_attention(dils, batch, seq):
    parts = [_dilated_branch(dil, d, batch, seq) for dil, d in zip(dils, DIL_DILATIONS)]
    tm = 1024
    spec = pl.BlockSpec((DIL_HEADS // 2, tm, LANES), lambda i: (0, i, 0))
    return pl.pallas_call(
        _dilated_combine_kernel,
        grid=(batch * seq // tm,),
        in_specs=[spec] * 6,
        out_specs=pl.BlockSpec((tm, GROUP_WIDTH), lambda i: (i, 0)),
        out_shape=jax.ShapeDtypeStruct((batch * seq, GROUP_WIDTH), _MXU_DTYPE),
        compiler_params=_params(("parallel",)),
        name="dilated_combine",
    )(parts[0][0], parts[1][0], parts[2][0], parts[0][1], parts[1][1], parts[2][1])


def _outproj_kernel(a_ref, b_ref, c_ref, d_ref, w_ref, x_ref, g_ref, o_ref):
    gw = GROUP_WIDTH
    acc = jnp.dot(a_ref[...], w_ref[0:gw, :], preferred_element_type=F32)
    acc += jnp.dot(b_ref[...], w_ref[gw:2 * gw, :], preferred_element_type=F32)
    acc += jnp.dot(c_ref[...], w_ref[2 * gw:3 * gw, :], preferred_element_type=F32)
    acc += jnp.dot(d_ref[...], w_ref[3 * gw:4 * gw, :], preferred_element_type=F32)
    o_ref[...] = x_ref[...] + _rms(acc, g_ref[...])


def _outproj(parts, w_out, x, gain):
    t, d = x.shape
    tm = 512
    part_spec = pl.BlockSpec((tm, GROUP_WIDTH), lambda i: (i, 0))
    row_spec = pl.BlockSpec((tm, d), lambda i: (i, 0))
    return pl.pallas_call(
        _outproj_kernel,
        grid=(t // tm,),
        in_specs=[part_spec] * 4 + [pl.BlockSpec((d, d), lambda i: (0, 0)), row_spec,
                                    pl.BlockSpec((1, d), lambda i: (0, 0))],
        out_specs=row_spec,
        out_shape=jax.ShapeDtypeStruct((t, d), F32),
        compiler_params=_params(("parallel",)),
        name="outproj",
    )(*parts, w_out, x, gain)


def _ffn_kernel(x_ref, gpre_ref, wg_ref, wu_ref, wd_ref, gpost_ref, o_ref, h_ref):
    k = pl.program_id(1)

    @pl.when(k == 0)
    def _():
        h_ref[...] = _rms(x_ref[...], gpre_ref[...]).astype(h_ref.dtype)
        o_ref[...] = jnp.zeros_like(o_ref)

    h = h_ref[...]
    gate = jnp.dot(h, wg_ref[...], preferred_element_type=F32)
    up = jnp.dot(h, wu_ref[...], preferred_element_type=F32)
    o_ref[...] += jnp.dot((_silu(gate) * up).astype(_MXU_DTYPE), wd_ref[...], preferred_element_type=F32)

    @pl.when(k == pl.num_programs(1) - 1)
    def _():
        o_ref[...] = x_ref[...] + _rms(o_ref[...], gpost_ref[...])


def _ffn(x, gpre, w_gate, w_up, w_down, gpost):
    t, d = x.shape
    hidden = w_gate.shape[1]
    tm, tk = 512, 512
    row_spec = pl.BlockSpec((tm, d), lambda i, k: (i, 0))
    vec_spec = pl.BlockSpec((1, d), lambda i, k: (0, 0))
    return pl.pallas_call(
        _ffn_kernel,
        grid=(t // tm, hidden // tk),
        in_specs=[row_spec, vec_spec,
                  pl.BlockSpec((d, tk), lambda i, k: (0, k)),
                  pl.BlockSpec((d, tk), lambda i, k: (0, k)),
                  pl.BlockSpec((tk, d), lambda i, k: (k, 0)),
                  vec_spec],
        out_specs=row_spec,
        out_shape=jax.ShapeDtypeStruct((t, d), F32),
        scratch_shapes=[pltpu.VMEM((tm, d), _MXU_DTYPE)],
        compiler_params=_params(("parallel", "arbitrary")),
        name="ffn",
    )(x, gpre, w_gate, w_up, w_down, gpost)


def _split_w_in(w_in):
    sizes = (256, 256, 512, 512, 512, SSD_CONV_DIM, SSD_HEADS, 512, 512, 512, 512, 512, 512)
    offs = np.cumsum((0,) + sizes)
    (r_q, r_k, r_v, r_g, s_z, s_xbc, s_dt, d_q, d_k, d_v, l_q, l_k, l_v) = [
        w_in[..., offs[i]:offs[i + 1]].astype(_MXU_DTYPE) for i in range(len(sizes))]
    w_qkv = jnp.concatenate([r_q, r_k * RET_DK ** -0.5, r_v, d_q * DIFF_HD ** -0.5, d_k, d_v], axis=-1)
    w_dil = jnp.concatenate([l_q * DIL_HD ** -0.5, l_k, l_v], axis=-1)
    pad = jnp.zeros(w_in.shape[:-1] + (GATE_COLS - _DT_SSD - SSD_HEADS,), _MXU_DTYPE)
    w_gates = jnp.concatenate([r_g, s_z, s_xbc, s_dt, pad], axis=-1)
    return w_qkv, w_dil, w_gates


def _mixer(x2, layer, gain_pre, w_in_parts, ret_norm, conv_w, conv_b, dt_bias, a_log, d_skip, ssd_norm,
           diff_lambda, diff_norm, batch, seq):
    w_qkv, w_dil, w_gates = w_in_parts
    qkv = _norm_matmul(x2, gain_pre, w_qkv, _MXU_DTYPE)
    dils = _dil_proj(x2, gain_pre, w_dil)
    gates = _norm_matmul(x2, gain_pre, w_gates, F32)
    out_a = _retention(qkv, gates, ret_norm, batch, seq)
    out_b = _ssd(gates, conv_w, conv_b, dt_bias, a_log, d_skip, ssd_norm, batch, seq)
    lam_init = 0.8 - 0.6 * math.exp(-0.3 * layer)
    out_c = _diff_attention(qkv, diff_lambda, diff_norm, lam_init, batch, seq)
    out_d = _dilated_attention(dils, batch, seq)
    return out_a, out_b, out_c, out_d


def kernel(x, norm_mix_pre, w_in, ret_norm, ssd_conv_w, ssd_conv_b, ssd_dt_bias, ssd_a_log, ssd_d, ssd_norm, diff_lambda, diff_norm, w_out, norm_mix_post, norm_ffn_pre, w_gate, w_up, w_down, norm_ffn_post):
    batch, seq, d = x.shape
    assert d == D_MODEL and seq % DIL_SPAN == 0
    depth = w_in.shape[0]
    x2 = x.reshape(batch * seq, d)
    vec = lambda g: g.reshape(1, d).astype(F32)
    w_in_parts = _split_w_in(w_in)
    w_out, w_gate, w_up, w_down = (w.astype(_MXU_DTYPE) for w in (w_out, w_gate, w_up, w_down))
    for l in range(depth):
        parts = _mixer(x2, l, vec(norm_mix_pre[l]), [w[l] for w in w_in_parts], ret_norm[l], ssd_conv_w[l],
                       ssd_conv_b[l], ssd_dt_bias[l], ssd_a_log[l], ssd_d[l], ssd_norm[l], diff_lambda[l],
                       diff_norm[l], batch, seq)
        x2 = _outproj(parts, w_out[l], x2, vec(norm_mix_post[l]))
        x2 = _ffn(x2, vec(norm_ffn_pre[l]), w_gate[l], w_up[l], w_down[l], vec(norm_ffn_post[l]))
    return x2.reshape(batch, seq, d)
```
